```python
import jax, jax.numpy as jnp
from jax import lax
import numpy as np

D_MODEL = 1024
BATCH = 8
SEQ = 2048
DEPTH = 4
DEC_BATCH = 128
DEC_SEQ = 1
PAST_LEN = 2048
PAGE_SIZE = 128

N_MIXERS = 2
N_A_LAYERS = (DEPTH + N_MIXERS - 1) // N_MIXERS
N_B_LAYERS = DEPTH // N_MIXERS
HEAD_DIM = 64
A_GROUPS = ((128, 1), (512, 4), (2048, 16))
N_A_GROUPS = len(A_GROUPS)
A_HEADS = D_MODEL // HEAD_DIM
A_WIDTH = A_HEADS * HEAD_DIM
A_IN_COLS = N_A_GROUPS * 3 * A_WIDTH + A_WIDTH
B_HEADS = D_MODEL // HEAD_DIM
B_KV_HEADS = 4
B_GROUP = B_HEADS // B_KV_HEADS
B_WIDTH = B_HEADS * HEAD_DIM
B_KV_WIDTH = B_KV_HEADS * HEAD_DIM
CMP_LEN = 32
CMP_STRIDE = 16
CMP_HIDDEN = 4 * HEAD_DIM
SEL_BLOCK = 64
SEL_TOPK = 16
WIN_B = 512
B_IN_COLS = B_WIDTH + 6 * B_KV_WIDTH + 3 * B_HEADS + 3 * B_WIDTH
QBLOCK = 128
SEL_QBLOCK = 32
DN_ALPHA = (2 * DEPTH) ** 0.25
DN_BETA = (8 * DEPTH) ** -0.25
LN_EPS = 1e-5
NEG = -1e30
FORCE_BONUS = 1e4
F32 = jnp.float32

kernel_name = "hybrid_dilated_nsa_decoder_step"


def alibi_slopes(n):
    return 2.0 ** (-8.0 * np.arange(1, n + 1, dtype=np.float32) / n)


def a_slopes():
    return jnp.asarray(alibi_slopes(N_A_GROUPS * A_HEADS).reshape(A_HEADS, N_A_GROUPS).T, F32)


def b_slopes():
    return jnp.asarray(alibi_slopes(B_HEADS).reshape(B_KV_HEADS, B_GROUP), F32)


def deepnorm(x, f, g, b):
    z = DN_ALPHA * x.astype(F32) + f.astype(F32)
    mu = z.mean(-1, keepdims=True)
    var = jnp.square(z - mu).mean(-1, keepdims=True)
    return ((z - mu) * lax.rsqrt(var + LN_EPS) * g.astype(F32) + b.astype(F32)).astype(x.dtype)


def masked_softmax(s, valid):
    s = jnp.where(valid, s, NEG)
    m = jnp.max(s, axis=-1, keepdims=True)
    e = jnp.where(valid, jnp.exp(s - m), 0.0)
    den = jnp.maximum(e.sum(-1, keepdims=True), 1e-30)
    return e / den, (m + jnp.log(den))[..., 0]


def attend_shared(q, k, v, dist, valid, slopes):
    s = jnp.einsum('bqgrd,bkgd->bgrqk', q.astype(F32), k.astype(F32)) * q.shape[-1] ** -0.5
    s = s - slopes[:, :, None, None] * dist.astype(F32)
    p, lse = masked_softmax(s, valid)
    o = jnp.einsum('bgrqk,bkgd->bqgrd', p, v.astype(F32))
    return o, lse.transpose(0, 3, 1, 2)


def attend_gathered(q, k, v, dist, valid, slopes):
    s = jnp.einsum('bqgrd,bqkgd->bgrqk', q.astype(F32), k.astype(F32)) * q.shape[-1] ** -0.5
    s = s - slopes[:, :, None, None] * dist.astype(F32)
    p, lse = masked_softmax(s, valid)
    o = jnp.einsum('bgrqk,bqkgd->bqgrd', p, v.astype(F32))
    return o, lse.transpose(0, 3, 1, 2)


def banded_attention(q, k, v, window, slopes, dist_scale):
    B, L, G, R, hd = q.shape
    nb = L // QBLOCK
    npre = -(-window // QBLOCK)
    span = (npre + 1) * QBLOCK
    pad = ((0, 0), (npre * QBLOCK, 0), (0, 0), (0, 0))
    kp, vp = jnp.pad(k, pad), jnp.pad(v, pad)
    rel = jnp.arange(QBLOCK)[:, None] + npre * QBLOCK - jnp.arange(span)[None, :]

    def block(b):
        s0 = b * QBLOCK
        qb = lax.dynamic_slice_in_dim(q, s0, QBLOCK, axis=1)
        kb = lax.dynamic_slice_in_dim(kp, s0, span, axis=1)
        vb = lax.dynamic_slice_in_dim(vp, s0, span, axis=1)
        key_pos = s0 - npre * QBLOCK + jnp.arange(span)
        valid = (rel >= 0) & (rel <= window) & (key_pos[None, :] >= 0)
        return attend_shared(qb, kb, vb, rel * dist_scale, valid, slopes)

    o, lse = lax.map(block, jnp.arange(nb))
    o = o.transpose(1, 0, 2, 3, 4, 5).reshape(B, L, G, R, hd)
    lse = lse.transpose(1, 0, 2, 3, 4).reshape(B, L, G, R)
    return o, lse


def a_merge(outs, lses, gate, w_out):
    wts = jax.nn.softmax(jnp.stack(lses), axis=0)
    o = jnp.sum(wts[..., None] * jnp.stack(outs), axis=0)
    o = o.reshape(o.shape[0], o.shape[1], A_WIDTH) * jax.nn.silu(gate.astype(F32))
    return o.astype(w_out.dtype) @ w_out


def mixer_a_prompt(h, w_in, w_out):
    B, S, _ = h.shape
    proj = h @ w_in
    qkv = proj[..., :N_A_GROUPS * 3 * A_WIDTH].reshape(B, S, N_A_GROUPS, 3, A_HEADS, HEAD_DIM)
    gate = proj[..., N_A_GROUPS * 3 * A_WIDTH:]
    slopes = a_slopes()
    outs, lses, states = [], [], []
    for g, (win, dil) in enumerate(A_GROUPS):
        n_cls = S // dil
        n_pad = -n_cls % QBLOCK

        def to_classes(t):
            t = t.reshape(B, n_cls, dil, A_HEADS, HEAD_DIM).transpose(0, 2, 1, 3, 4)
            t = t.reshape(B * dil, n_cls, A_HEADS, HEAD_DIM)
            return jnp.pad(t, ((0, 0), (0, n_pad), (0, 0), (0, 0)))

        qc, kc, vc = to_classes(qkv[:, :, g, 0]), to_classes(qkv[:, :, g, 1]), to_classes(qkv[:, :, g, 2])
        o, lse = banded_attention(qc[:, :, :, None], kc, vc, win // dil, slopes[g][:, None], dil)
        o = o[:, :n_cls, :, 0].reshape(B, dil, n_cls, A_HEADS, HEAD_DIM).transpose(0, 2, 1, 3, 4)
        lse = lse[:, :n_cls, :, 0].reshape(B, dil, n_cls, A_HEADS).transpose(0, 2, 1, 3)
        outs.append(o.reshape(B, S, A_HEADS, HEAD_DIM))
        lses.append(lse.reshape(B, S, A_HEADS))
        states.append(qkv[:, S - min(win, S):, g, 1:])
    return a_merge(outs, lses, gate, w_out).astype(h.dtype), states


def mixer_a_sample(h, bufs, w_in, w_out):
    Bd, T, _ = h.shape
    proj = h @ w_in
    qkv = proj[..., :N_A_GROUPS * 3 * A_WIDTH].reshape(Bd, T, N_A_GROUPS, 3, A_HEADS, HEAD_DIM)
    gate = proj[..., N_A_GROUPS * 3 * A_WIDTH:]
    slopes = a_slopes()
    outs, lses, states = [], [], []
    for g, (win, dil) in enumerate(A_GROUPS):
        buf = bufs[g]
        n_buf = buf.shape[1]
        full = jnp.concatenate([buf, qkv[:, :, g, 1:].astype(buf.dtype)], axis=1)
        steps = jnp.arange(win // dil + 1)
        idx = n_buf + jnp.arange(T)[:, None] - steps[None, :] * dil
        valid = idx >= 0
        kv_g = full[:, jnp.maximum(idx, 0)]
        o, lse = attend_gathered(qkv[:, :, g, 0][:, :, :, None], kv_g[:, :, :, 0], kv_g[:, :, :, 1],
                                 steps[None, :] * dil, valid, slopes[g][:, None])
        outs.append(o[:, :, :, 0])
        lses.append(lse[:, :, :, 0])
        states.append(full[:, n_buf + T - min(win, n_buf + T):])
    return a_merge(outs, lses, gate, w_out).astype(h.dtype), states


def b_project(h, w_in):
    B, S, _ = h.shape
    proj = h @ w_in
    o0 = B_WIDTH
    o1 = o0 + 6 * B_KV_WIDTH
    o2 = o1 + 3 * B_HEADS
    q = proj[..., :o0].reshape(B, S, B_KV_HEADS, B_GROUP, HEAD_DIM)
    kv = proj[..., o0:o1].reshape(B, S, 6, B_KV_HEADS, HEAD_DIM)
    gates = jax.nn.sigmoid(proj[..., o1:o2].astype(F32)).reshape(B, S, 3, B_KV_HEADS, B_GROUP)
    gpath = proj[..., o2:].reshape(B, S, 3, B_WIDTH)
    return q, kv, gates, gpath


def compress(rows, pos, w1, w2):
    B, L, G, hd = rows.shape
    n_cmp = (L - CMP_LEN) // CMP_STRIDE + 1
    idx = np.arange(n_cmp)[:, None] * CMP_STRIDE + np.arange(CMP_LEN)[None, :]
    blocks = rows[:, idx] + pos[:, None, :]
    flat = blocks.transpose(0, 1, 3, 2, 4).reshape(B, n_cmp, G, CMP_LEN * hd)
    return jax.nn.gelu(flat @ w1) @ w2


def sel_aggregation(n_cmp, n_sel):
    r_sel, r_cmp = SEL_BLOCK // CMP_STRIDE, CMP_LEN // CMP_STRIDE
    j, m, n = np.meshgrid(np.arange(n_sel), np.arange(r_sel), np.arange(r_cmp), indexing='ij')
    i = (r_sel * j + m - n).ravel()
    jj = j.ravel()
    keep = (i >= 0) & (i < n_cmp)
    agg = np.zeros((n_cmp, n_sel), np.float32)
    np.add.at(agg, (i[keep], jj[keep]), 1.0)
    return jnp.asarray(agg)


def nsa_cmp_sel(q, pos_q, k_cmp, v_cmp, k_rows, v_rows, slopes, agg):
    B, Q, G, R, hd = q.shape
    n_cmp = k_cmp.shape[1]
    n_sel = k_rows.shape[1] // SEL_BLOCK
    scale = hd ** -0.5
    qf = q.astype(F32)
    cmp_end = jnp.arange(n_cmp) * CMP_STRIDE + CMP_LEN - 1
    dist_c = pos_q[:, None] - cmp_end[None, :]
    s_c = jnp.einsum('bqgrd,bkgd->bgrqk', qf, k_cmp.astype(F32)) * scale
    s_c = s_c - slopes[:, :, None, None] * dist_c.astype(F32)
    p_c, _ = masked_softmax(s_c, dist_c >= 0)
    o_cmp = jnp.einsum('bgrqk,bkgd->bqgrd', p_c, v_cmp.astype(F32))
    s_blk = jnp.einsum('bgrqk,kn->bgqn', p_c, agg)
    blk = jnp.arange(n_sel)[None, :]
    cur = (pos_q // SEL_BLOCK)[:, None]
    forced = (blk == 0) | (blk == cur) | (blk == cur - 1)
    s_blk = jnp.where(forced, s_blk + FORCE_BONUS, s_blk)
    s_blk = jnp.where(blk * SEL_BLOCK <= pos_q[:, None], s_blk, NEG)
    _, sel = lax.top_k(s_blk, min(SEL_TOPK, n_sel))
    kb = k_rows.reshape(B, n_sel, SEL_BLOCK, G, hd).transpose(0, 3, 1, 2, 4)
    vb = v_rows.reshape(B, n_sel, SEL_BLOCK, G, hd).transpose(0, 3, 1, 2, 4)
    gather = jax.vmap(jax.vmap(lambda t, i: t[i]))
    k_sel, v_sel = gather(kb, sel), gather(vb, sel)
    key_pos = sel[..., None] * SEL_BLOCK + jnp.arange(SEL_BLOCK)
    dist_s = pos_q[None, None, :, None, None] - key_pos
    s_s = jnp.einsum('bqgrd,bgqknd->bgrqkn', qf, k_sel.astype(F32)) * scale
    s_s = s_s - slopes[None, :, :, None, None, None] * dist_s[:, :, None].astype(F32)
    nk = s_s.shape[-2] * SEL_BLOCK
    p_s, _ = masked_softmax(s_s.reshape(B, G, R, Q, nk), (dist_s >= 0)[:, :, None].reshape(B, G, 1, Q, nk))
    o_sel = jnp.einsum('bgrqn,bgqnd->bqgrd', p_s, v_sel.reshape(B, G, Q, nk, hd).astype(F32))
    return o_cmp, o_sel


def nsa_merge(o_cmp, o_sel, o_win, gates, gpath, w_out):
    B, S = gates.shape[:2]
    o = jnp.stack([o_cmp, o_sel, o_win], axis=2) * gates[..., None]
    o = o.reshape(B, S, 3, B_WIDTH) * jax.nn.silu(gpath.astype(F32))
    return o.sum(axis=2).astype(w_out.dtype) @ w_out


def mixer_b_prompt(h, w_in, w_out, phi_pos, phi_w1, phi_w2):
    B, S, _ = h.shape
    q, kv, gates, gpath = b_project(h, w_in)
    slopes = b_slopes()
    k_cmp = compress(kv[:, :, 0], phi_pos[0], phi_w1[0], phi_w2[0])
    v_cmp = compress(kv[:, :, 1], phi_pos[1], phi_w1[1], phi_w2[1])
    agg = sel_aggregation(k_cmp.shape[1], S // SEL_BLOCK)
    k_rows, v_rows = kv[:, :, 2], kv[:, :, 3]

    def block(b):
        s0 = b * SEL_QBLOCK
        qb = lax.dynamic_slice_in_dim(q, s0, SEL_QBLOCK, axis=1)
        return nsa_cmp_sel(qb, s0 + jnp.arange(SEL_QBLOCK), k_cmp, v_cmp, k_rows, v_rows, slopes, agg)

    o_cmp, o_sel = lax.map(block, jnp.arange(S // SEL_QBLOCK))
    o_cmp = o_cmp.transpose(1, 0, 2, 3, 4, 5).reshape(q.shape)
    o_sel = o_sel.transpose(1, 0, 2, 3, 4, 5).reshape(q.shape)
    o_win, _ = banded_attention(q, kv[:, :, 4], kv[:, :, 5], WIN_B, slopes, 1)
    y = nsa_merge(o_cmp, o_sel, o_win, gates, gpath, w_out)
    return y.astype(h.dtype), kv[:, :, :4], kv[:, S - min(WIN_B, S):, 4:]


def mixer_b_sample(h, pages, win_buf, page_table, w_in, w_out, phi_pos, phi_w1, phi_w2):
    Bd, T, _ = h.shape
    q, kv, gates, gpath = b_project(h, w_in)
    slopes = b_slopes()
    new_rows = kv[:, :, :4]
    past = pages[page_table]
    past = past.reshape(Bd, past.shape[1] * past.shape[2], 4, B_KV_HEADS, HEAD_DIM)
    n_past = past.shape[1]
    rows = jnp.concatenate([past, new_rows.astype(past.dtype)], axis=1)
    L = rows.shape[1]
    k_cmp = compress(rows[:, :, 0], phi_pos[0], phi_w1[0], phi_w2[0])
    v_cmp = compress(rows[:, :, 1], phi_pos[1], phi_w1[1], phi_w2[1])
    n_sel = -(-L // SEL_BLOCK)
    pad = ((0, 0), (0, n_sel * SEL_BLOCK - L), (0, 0), (0, 0))
    k_rows, v_rows = jnp.pad(rows[:, :, 2], pad), jnp.pad(rows[:, :, 3], pad)
    agg = sel_aggregation(k_cmp.shape[1], n_sel)
    o_cmp, o_sel = nsa_cmp_sel(q, n_past + jnp.arange(T), k_cmp, v_cmp, k_rows, v_rows, slopes, agg)
    n_buf = win_buf.shape[1]
    full = jnp.concatenate([win_buf, kv[:, :, 4:].astype(win_buf.dtype)], axis=1)
    dist = (n_buf + jnp.arange(T))[:, None] - jnp.arange(n_buf + T)[None, :]
    valid = (dist >= 0) & (dist <= WIN_B)
    o_win, _ = attend_shared(q, full[:, :, 0], full[:, :, 1], dist, valid, slopes)
    y = nsa_merge(o_cmp, o_sel, o_win, gates, gpath, w_out)
    new_win = full[:, n_buf + T - min(WIN_B, n_past + T):]
    return y.astype(h.dtype), new_rows, new_win


def setup_inputs(seed: int = 0) -> dict:
    key = jax.random.key(seed)
    ks = jax.random.split(key, 20)
    n_pages = PAST_LEN // PAGE_SIZE
    n_pool = (5 * DEC_BATCH * n_pages + 3) // 4

    def nrm(k, shape, scale):
        return jax.random.normal(k, shape, F32) * scale

    kv2_scale = jnp.asarray(np.array([1.0, DN_BETA], np.float32))[:, None, None]
    kv4_scale = jnp.asarray(np.array([1.0, DN_BETA, 1.0, DN_BETA], np.float32))[:, None, None]
    a_bufs = [nrm(ks[2 + g], (N_A_LAYERS, DEC_BATCH, min(w, PAST_LEN), 2, A_HEADS, HEAD_DIM), 1.0) * kv2_scale
              for g, (w, _) in enumerate(A_GROUPS)]
    cache_b_pages = nrm(ks[5], (N_B_LAYERS, n_pool, PAGE_SIZE, 4, B_KV_HEADS, HEAD_DIM), 1.0) * kv4_scale
    cache_b_win = nrm(ks[6], (N_B_LAYERS, DEC_BATCH, min(WIN_B, PAST_LEN), 2, B_KV_HEADS, HEAD_DIM), 1.0) * kv2_scale
    page_table = jax.random.permutation(ks[7], n_pool)[:DEC_BATCH * n_pages].reshape(DEC_BATCH, n_pages).astype(jnp.int32)

    a_col = np.ones((A_IN_COLS,), np.float32)
    a_col[:N_A_GROUPS * 3 * A_WIDTH].reshape(N_A_GROUPS, 3, A_WIDTH)[:, 2] = DN_BETA
    b_col = np.ones((B_IN_COLS,), np.float32)
    b_col[B_WIDTH:B_WIDTH + 6 * B_KV_WIDTH].reshape(6, B_KV_WIDTH)[1::2] = DN_BETA
    a_w_in = nrm(ks[8], (N_A_LAYERS, D_MODEL, A_IN_COLS), D_MODEL ** -0.5) * jnp.asarray(a_col)
    a_w_out = nrm(ks[9], (N_A_LAYERS, A_WIDTH, D_MODEL), A_WIDTH ** -0.5 * DN_BETA)
    b_w_in = nrm(ks[10], (N_B_LAYERS, D_MODEL, B_IN_COLS), D_MODEL ** -0.5) * jnp.asarray(b_col)
    b_w_out = nrm(ks[11], (N_B_LAYERS, B_WIDTH, D_MODEL), B_WIDTH ** -0.5 * DN_BETA)
    return {
        "x_prompt": nrm(ks[0], (BATCH, SEQ, D_MODEL), 1.0),
        "x_sample": nrm(ks[1], (DEC_BATCH, DEC_SEQ, D_MODEL), 1.0),
        "cache_a_w128": a_bufs[0],
        "cache_a_w512": a_bufs[1],
        "cache_a_w2048": a_bufs[2],
        "cache_b_pages": cache_b_pages,
        "cache_b_win": cache_b_win,
        "page_table": page_table,
        "ln_g": 1.0 + nrm(ks[12], (DEPTH, D_MODEL), 0.02),
        "ln_b": nrm(ks[13], (DEPTH, D_MODEL), 0.02),
        "a_w_in": a_w_in,
        "a_w_out": a_w_out,
        "b_w_in": b_w_in,
        "b_w_out": b_w_out,
        "b_phi_pos": nrm(ks[14], (N_B_LAYERS, 2, CMP_LEN, HEAD_DIM), 0.1),
        "b_phi_w1": nrm(ks[15], (N_B_LAYERS, 2, CMP_LEN * HEAD_DIM, CMP_HIDDEN), (CMP_LEN * HEAD_DIM) ** -0.5),
        "b_phi_w2": nrm(ks[16], (N_B_LAYERS, 2, CMP_HIDDEN, HEAD_DIM), CMP_HIDDEN ** -0.5),
    }


def reference(x_prompt, x_sample, cache_a_w128, cache_a_w512, cache_a_w2048, cache_b_pages, cache_b_win,
              page_table, ln_g, ln_b, a_w_in, a_w_out, b_w_in, b_w_out, b_phi_pos, b_phi_w1, b_phi_w2):
    a_bufs = (cache_a_w128, cache_a_w512, cache_a_w2048)
    yp, ys = x_prompt, x_sample
    a_p = [[] for _ in A_GROUPS]
    a_s = [[] for _ in A_GROUPS]
    rows_p, rows_s, win_p, win_s = [], [], [], []
    for i in range(DEPTH):
        j = i // N_MIXERS
        if i % N_MIXERS == 0:
            dp, st_p = mixer_a_prompt(yp, a_w_in[j], a_w_out[j])
            ds, st_s = mixer_a_sample(ys, [c[j] for c in a_bufs], a_w_in[j], a_w_out[j])
            for g in range(N_A_GROUPS):
                a_p[g].append(st_p[g])
                a_s[g].append(st_s[g])
        else:
            dp, rp, wp = mixer_b_prompt(yp, b_w_in[j], b_w_out[j], b_phi_pos[j], b_phi_w1[j], b_phi_w2[j])
            ds, rs, ws = mixer_b_sample(ys, cache_b_pages[j], cache_b_win[j], page_table, b_w_in[j], b_w_out[j],
                                        b_phi_pos[j], b_phi_w1[j], b_phi_w2[j])
            rows_p.append(rp)
            rows_s.append(rs)
            win_p.append(wp)
            win_s.append(ws)
        yp = deepnorm(yp, dp, ln_g[i], ln_b[i])
        ys = deepnorm(ys, ds, ln_g[i], ln_b[i])
    return (yp, ys,
            jnp.stack(a_p[0]), jnp.stack(a_s[0]),
            jnp.stack(a_p[1]), jnp.stack(a_s[1]),
            jnp.stack(a_p[2]), jnp.stack(a_s[2]),
            jnp.stack(rows_p), jnp.stack(rows_s),
            jnp.stack(win_p), jnp.stack(win_s))
```

```python
import functools

import numpy as np
import jax
import jax.numpy as jnp
from jax import lax
from jax.experimental import pallas as pl
from jax.experimental.pallas import tpu as pltpu

F32 = jnp.float32
BF16 = jnp.bfloat16

D_MODEL = 1024
DEPTH = 4
HEAD_DIM = 64
N_HEADS = D_MODEL // HEAD_DIM
A_GROUPS = ((128, 1), (512, 4), (2048, 16))
A_CLASS_WINDOW = 128
A_QKV_COLS = 3 * 3 * D_MODEL
B_KV_HEADS = 4
B_GROUP = N_HEADS // B_KV_HEADS
B_KV_WIDTH = B_KV_HEADS * HEAD_DIM
CMP_LEN = 32
CMP_STRIDE = 16
CMP_HIDDEN = 4 * HEAD_DIM
SEL_BLOCK = 64
SEL_TOPK = 16
WIN_B = 512
DN_ALPHA = (2 * DEPTH) ** 0.25
LN_EPS = 1e-5
NEG = -1e30
FORCE_BONUS = 1e4
MASKED_DIST = 1e30
SCALE = HEAD_DIM ** -0.5

LANES = 128
VMEM_LIMIT = 56 * 1024 * 1024


def _alibi_slopes(n):
    return 2.0 ** (-8.0 * np.arange(1, n + 1, dtype=np.float32) / n)


def _params(sem, vmem=VMEM_LIMIT):
    return pltpu.CompilerParams(dimension_semantics=sem, vmem_limit_bytes=vmem)


def _dot(a, b):
    return jnp.dot(a, b, preferred_element_type=F32)


def _dot_nt(a, b):
    return lax.dot_general(a, b, (((1,), (1,)), ((), ())), preferred_element_type=F32)


def _silu(x):
    return x / (1.0 + jnp.exp(-x))


def _sigmoid(x):
    return 1.0 / (1.0 + jnp.exp(-x))


def _matmul_kernel(x_ref, w_ref, o_ref):
    o_ref[...] = _dot(x_ref[...].astype(BF16), w_ref[...])


def _matmul(x, w, *, tm, tn, name):
    m, k = x.shape
    n = w.shape[1]
    return pl.pallas_call(
        _matmul_kernel,
        grid=(m // tm, n // tn),
        in_specs=[pl.BlockSpec((tm, k), lambda i, j: (i, 0)),
                  pl.BlockSpec((k, tn), lambda i, j: (0, j))],
        out_specs=pl.BlockSpec((tm, tn), lambda i, j: (i, j)),
        out_shape=jax.ShapeDtypeStruct((m, n), F32),
        compiler_params=_params(("parallel", "arbitrary")),
        name=name,
    )(x, w)


def _outproj_ln_kernel(o_ref, x_ref, w_ref, g_ref, b_ref, y_ref):
    f = _dot(o_ref[...].astype(BF16), w_ref[...])
    z = DN_ALPHA * x_ref[...] + f
    mu = jnp.mean(z, axis=-1, keepdims=True)
    zc = z - mu
    var = jnp.mean(zc * zc, axis=-1, keepdims=True)
    y_ref[...] = zc * lax.rsqrt(var + LN_EPS) * g_ref[...] + b_ref[...]


def _outproj_ln(o, x, w, g, b, *, tm, name):
    m = o.shape[0]
    row = lambda i: (i, 0)
    fixed = lambda i: (0, 0)
    return pl.pallas_call(
        _outproj_ln_kernel,
        grid=(m // tm,),
        in_specs=[pl.BlockSpec((tm, D_MODEL), row), pl.BlockSpec((tm, D_MODEL), row),
                  pl.BlockSpec((D_MODEL, D_MODEL), fixed),
                  pl.BlockSpec((1, D_MODEL), fixed), pl.BlockSpec((1, D_MODEL), fixed)],
        out_specs=pl.BlockSpec((tm, D_MODEL), row),
        out_shape=jax.ShapeDtypeStruct((m, D_MODEL), F32),
        compiler_params=_params(("parallel",)),
        name=name,
    )(o, x, w, g.reshape(1, D_MODEL), b.reshape(1, D_MODEL))


A_TILE = 128


def _a_pair_tile(q2, k2, v2, dist_m, sd0, sd1, lane_lo):
    qs = (q2 * SCALE)
    q0 = jnp.where(lane_lo, qs, 0.0).astype(BF16)
    q1 = jnp.where(lane_lo, 0.0, qs).astype(BF16)
    kb = k2.astype(BF16)
    vb = v2.astype(BF16)
    outs, lses = [], []
    for qh, sd in ((q0, sd0), (q1, sd1)):
        s = _dot_nt(qh, kb) - sd * dist_m
        m = jnp.max(s, axis=-1, keepdims=True)
        e = jnp.exp(s - m)
        den = jnp.maximum(jnp.sum(e, axis=-1, keepdims=True), 1e-30)
        outs.append(_dot(e.astype(BF16), vb) / den)
        lses.append(m + jnp.log(den))
    o = jnp.where(lane_lo, outs[0], outs[1])
    lse = jnp.where(lane_lo, lses[0], lses[1])
    return o, lse


def _a_prompt_kernel(slopes_ref, q0_ref, k0_ref, v0_ref, q1_ref, k1_ref, v1_ref,
                     q2_ref, k2_ref, v2_ref, gate_ref, o_ref, acc_ref, lse_ref, *, seq):
    hp = pl.program_id(1)
    lane_lo = lax.broadcasted_iota(jnp.int32, (A_TILE, LANES), 1) < HEAD_DIM
    refs = ((q0_ref, k0_ref, v0_ref), (q1_ref, k1_ref, v1_ref), (q2_ref, k2_ref, v2_ref))
    for g, (win, dil) in enumerate(A_GROUPS):
        q_ref, k_ref, v_ref = refs[g]
        n_cls = seq // dil
        n_tiles = n_cls // A_TILE
        tk = A_TILE if n_tiles == 1 else 2 * A_TILE
        sd0 = slopes_ref[g, 2 * hp] * dil
        sd1 = slopes_ref[g, 2 * hp + 1] * dil
        qi = lax.broadcasted_iota(jnp.int32, (A_TILE, tk), 0)
        ci = lax.broadcasted_iota(jnp.int32, (A_TILE, tk), 1)

        def body(idx, carry, q_ref=q_ref, k_ref=k_ref, v_ref=v_ref, dil=dil, n_tiles=n_tiles,
                 tk=tk, sd0=sd0, sd1=sd1, qi=qi, ci=ci, g=g):
            r = idx // n_tiles
            t = idx % n_tiles
            tprev = jnp.maximum(t - 1, 0)
            q_start = r + dil * A_TILE * t
            k_start = r + dil * A_TILE * tprev if n_tiles > 1 else r
            off = jnp.where(t > 0, A_TILE, 0) if n_tiles > 1 else 0
            d = off + qi - ci
            valid = (d >= 0) & (d <= A_CLASS_WINDOW)
            dist_m = jnp.where(valid, d.astype(F32), MASKED_DIST)
            if dil == 1:
                q2 = q_ref[pl.ds(pl.multiple_of(q_start, A_TILE), A_TILE), :]
                k2 = k_ref[pl.ds(pl.multiple_of(k_start, A_TILE), tk), :]
                v2 = v_ref[pl.ds(pl.multiple_of(k_start, A_TILE), tk), :]
            else:
                q2 = q_ref[pl.ds(q_start, A_TILE, stride=dil), :]
                k2 = k_ref[pl.ds(k_start, tk, stride=dil), :]
                v2 = v_ref[pl.ds(k_start, tk, stride=dil), :]
            o, lse = _a_pair_tile(q2, k2, v2, dist_m, sd0, sd1, lane_lo)
            if dil == 1:
                acc_ref[g, pl.ds(pl.multiple_of(q_start, A_TILE), A_TILE), :] = o
                lse_ref[g, pl.ds(pl.multiple_of(q_start, A_TILE), A_TILE), :] = lse
            else:
                acc_ref[g, pl.ds(q_start, A_TILE, stride=dil), :] = o
                lse_ref[g, pl.ds(q_start, A_TILE, stride=dil), :] = lse
            return carry

        lax.fori_loop(0, seq // A_TILE, body, 0)

    l0, l1, l2 = lse_ref[0], lse_ref[1], lse_ref[2]
    mx = jnp.maximum(jnp.maximum(l0, l1), l2)
    w0, w1, w2 = jnp.exp(l0 - mx), jnp.exp(l1 - mx), jnp.exp(l2 - mx)
    o = (w0 * acc_ref[0] + w1 * acc_ref[1] + w2 * acc_ref[2]) / (w0 + w1 + w2)
    o_ref[...] = o * _silu(gate_ref[...])


def _a_prompt_attention(proj, slopes, batch, seq):
    p3 = proj.reshape(batch, seq, proj.shape[-1])
    n_pairs = D_MODEL // LANES

    def col_spec(col0):
        blk = col0 // LANES
        return pl.BlockSpec((None, seq, LANES), lambda b, hp, blk=blk: (b, 0, blk + hp))

    in_specs = [pl.BlockSpec(memory_space=pltpu.SMEM)]
    for g in range(3):
        for t in range(3):
            in_specs.append(col_spec(g * 3 * D_MODEL + t * D_MODEL))
    in_specs.append(col_spec(A_QKV_COLS))
    out = pl.pallas_call(
        functools.partial(_a_prompt_kernel, seq=seq),
        grid=(batch, n_pairs),
        in_specs=in_specs,
        out_specs=pl.BlockSpec((None, seq, LANES), lambda b, hp: (b, 0, hp)),
        scratch_shapes=[pltpu.VMEM((3, seq, LANES), F32), pltpu.VMEM((3, seq, LANES), F32)],
        out_shape=jax.ShapeDtypeStruct((batch, seq, D_MODEL), F32),
        compiler_params=_params(("parallel", "arbitrary")),
        name="a_prompt_attention",
    )(slopes, *([p3] * 10))
    return out.reshape(batch * seq, D_MODEL)


A_DEC_BLOCK = 4


def _split_dot(x, w):
    hi = x.astype(BF16)
    lo = (x - hi.astype(F32)).astype(BF16)
    return _dot(hi, w) + _dot(lo, w)


def _a_sample_kernel(slopes_ref, proj_ref, c0_ref, c1_ref, c2_ref, eh_ref, eht_ref, o_ref):
    caches = (c0_ref, c1_ref, c2_ref)
    eh = eh_ref[...]
    eht = eht_ref[...]
    steps = (A_CLASS_WINDOW - lax.broadcasted_iota(jnp.int32, (A_CLASS_WINDOW, N_HEADS), 0)).astype(F32)

    def body(i, carry):
        outs, lses = [], []
        for g, (win, dil) in enumerate(A_GROUPS):
            c_ref = caches[g]
            base = g * 3 * D_MODEL
            q = proj_ref[pl.ds(i, 1), base:base + D_MODEL]
            kn = proj_ref[pl.ds(i, 1), base + D_MODEL:base + 2 * D_MODEL]
            vn = proj_ref[pl.ds(i, 1), base + 2 * D_MODEL:base + 3 * D_MODEL]
            kc = c_ref[i, :, 0:D_MODEL]
            vc = c_ref[i, :, D_MODEL:2 * D_MODEL]
            sd = slopes_ref[g:g + 1, :] * float(dil)
            s = _dot((kc * q).astype(BF16), eh) * SCALE - sd * steps
            s_new = _dot((kn * q).astype(BF16), eh) * SCALE
            m = jnp.maximum(jnp.max(s, axis=0, keepdims=True), s_new)
            e = jnp.exp(s - m)
            e_new = jnp.exp(s_new - m)
            den = jnp.maximum(jnp.sum(e, axis=0, keepdims=True) + e_new, 1e-30)
            p = _dot((e / den).astype(BF16), eht)
            p_new = _dot((e_new / den).astype(BF16), eht)
            outs.append(jnp.sum(p * vc, axis=0, keepdims=True) + p_new * vn)
            lses.append(m + jnp.log(den))
        mx = jnp.maximum(jnp.maximum(lses[0], lses[1]), lses[2])
        ws = [jnp.exp(l - mx) for l in lses]
        tot = ws[0] + ws[1] + ws[2]
        o = sum(_split_dot(w / tot, eht) * og for w, og in zip(ws, outs))
        gate = proj_ref[pl.ds(i, 1), A_QKV_COLS:A_QKV_COLS + D_MODEL]
        o_ref[pl.ds(i, 1), :] = o * _silu(gate)
        return carry

    lax.fori_loop(0, A_DEC_BLOCK, body, 0)


def _head_sum_matrices():
    lane_head = np.arange(D_MODEL) // HEAD_DIM
    eh = (lane_head[:, None] == np.arange(N_HEADS)[None, :]).astype(np.float32)
    return jnp.asarray(eh, BF16), jnp.asarray(eh.T, BF16)


def _a_sample_attention(proj_s, caches, layer, slopes):
    n = proj_s.shape[0]
    bb = A_DEC_BLOCK
    eh, eht = _head_sum_matrices()
    row_w = 2 * D_MODEL
    views, specs = [], []
    for (win, dil), c in zip(A_GROUPS, caches):
        views.append(c.reshape(c.shape[0] * n, c.shape[2] // dil, dil * row_w))
        specs.append(pl.BlockSpec((bb, A_CLASS_WINDOW, row_w),
                                  lambda i, base=layer * (n // bb): (base + i, 0, 0)))
    fixed = lambda i: (0, 0)
    out = pl.pallas_call(
        _a_sample_kernel,
        grid=(n // bb,),
        in_specs=[pl.BlockSpec((3, N_HEADS), fixed),
                  pl.BlockSpec((None, bb, proj_s.shape[1]), lambda i: (i, 0, 0)),
                  *specs,
                  pl.BlockSpec((D_MODEL, N_HEADS), fixed), pl.BlockSpec((N_HEADS, D_MODEL), fixed)],
        out_specs=pl.BlockSpec((None, bb, D_MODEL), lambda i: (i, 0, 0)),
        out_shape=jax.ShapeDtypeStruct((n // bb, bb, D_MODEL), F32),
        compiler_params=_params(("parallel",)),
        name="a_sample_attention",
    )(slopes, proj_s.reshape(n // bb, bb, proj_s.shape[1]), *views, eh, eht)
    return out.reshape(n, D_MODEL)


B_Q_COL = 0
B_GPATH_COL = D_MODEL
B_KV_COL = 4 * D_MODEL
B_GATE_COL = B_KV_COL + 6 * B_KV_WIDTH
B_COLS = B_GATE_COL + LANES
B_SLAB = B_KV_HEADS * HEAD_DIM
N_CMP_PAD = 128
N_SEL_PAD = 128


def _b_column_order():
    head_perm = np.arange(D_MODEL).reshape(B_KV_HEADS, B_GROUP, HEAD_DIM).transpose(1, 0, 2).reshape(-1)
    o_kv = D_MODEL
    o_gate = o_kv + 6 * B_KV_WIDTH
    o_gpath = o_gate + 3 * N_HEADS
    cols = [head_perm]
    for c in range(3):
        cols.append(o_gpath + c * D_MODEL + head_perm)
    cols.append(o_kv + np.arange(6 * B_KV_WIDTH))
    cols.append(o_gate + np.arange(3 * N_HEADS))
    cols.append(np.full((LANES - 3 * N_HEADS,), -1))
    return np.concatenate(cols), head_perm


def _gate_expand_matrix():
    m = np.zeros((LANES, 3 * D_MODEL), np.float32)
    for c in range(3):
        for g in range(B_KV_HEADS):
            for r in range(B_GROUP):
                lo = c * D_MODEL + r * B_SLAB + g * HEAD_DIM
                m[c * N_HEADS + g * B_GROUP + r, lo:lo + HEAD_DIM] = 1.0
    return jnp.asarray(m, BF16)


def _sel_aggregation(n_cmp, n_sel):
    r_sel, r_cmp = SEL_BLOCK // CMP_STRIDE, CMP_LEN // CMP_STRIDE
    agg = np.zeros((N_CMP_PAD, N_SEL_PAD), np.float32)
    for j in range(n_sel):
        for m in range(r_sel):
            for n in range(r_cmp):
                i = r_sel * j + m - n
                if 0 <= i < n_cmp:
                    agg[i, j] += 1.0
    return jnp.asarray(agg, BF16)


def _gelu_tanh(x):
    return 0.5 * x * (1.0 + jnp.tanh(0.7978845608028654 * (x + 0.044715 * (x * x * x))))


def _lane_group(shape):
    return lax.broadcasted_iota(jnp.int32, shape, 1) // HEAD_DIM


def _compress(load_rows, w1t_ref, posterm, w2wide):
    lane_g = _lane_group((N_CMP_PAD, LANES))
    w2_lane_g = _lane_group(w2wide.shape)
    out = jnp.zeros((N_CMP_PAD, B_SLAB), F32)
    for g in range(B_KV_HEADS):
        uv = jnp.zeros((N_CMP_PAD, 2 * CMP_HIDDEN), F32)
        for l in range(CMP_STRIDE):
            xg = jnp.where(lane_g == g % 2, load_rows(l, g // 2), 0.0).astype(BF16)
            uv = uv + _dot(xg, w1t_ref[l])
        nxt = pltpu.roll(uv[:, CMP_HIDDEN:], N_CMP_PAD - 1, 0)
        hid = _gelu_tanh(uv[:, :CMP_HIDDEN] + nxt + posterm)
        out = out + _dot(hid.astype(BF16), jnp.where(w2_lane_g == g, w2wide, 0.0).astype(BF16))
    return out


def _b_compress_kernel(xa_ref, xb_ref, w1t_ref, pos_ref, w1_ref, w2_ref, o_ref):
    posterm = _dot(pos_ref[...].astype(BF16), w1_ref[...])
    halves = (xa_ref, xb_ref)
    load_rows = lambda l, half: halves[half][pl.ds(l, N_CMP_PAD, stride=CMP_STRIDE), :]
    o_ref[...] = _compress(load_rows, w1t_ref, posterm, w2_ref[...])


def _compress_weights(phi_pos, phi_w1, phi_w2):
    w1 = phi_w1.astype(BF16)
    taps = w1.reshape(2, CMP_LEN, HEAD_DIM, CMP_HIDDEN)
    pair = jnp.concatenate([taps[:, :CMP_STRIDE], taps[:, CMP_STRIDE:]], axis=-1)
    w1t = jnp.tile(pair, (1, 1, LANES // HEAD_DIM, 1))
    pos = phi_pos.reshape(2, 1, CMP_LEN * HEAD_DIM)
    w2wide = jnp.tile(phi_w2, (1, 1, B_KV_HEADS))
    return w1t, pos, w1, w2wide


def _b_prompt_compress(p3, w1t, pos, w1, w2wide):
    batch, seq, _ = p3.shape
    kv_blk = B_KV_COL // LANES
    per_kind = lambda b, k: (k, 0, 0)
    half = lambda h: pl.BlockSpec((None, seq, LANES), lambda b, k, h=h: (b, 0, kv_blk + 2 * k + h))
    return pl.pallas_call(
        _b_compress_kernel,
        grid=(batch, 2),
        in_specs=[half(0), half(1),
                  pl.BlockSpec((None, CMP_STRIDE, LANES, 2 * CMP_HIDDEN), lambda b, k: (k, 0, 0, 0)),
                  pl.BlockSpec((None, 1, CMP_LEN * HEAD_DIM), per_kind),
                  pl.BlockSpec((None, CMP_LEN * HEAD_DIM, CMP_HIDDEN), per_kind),
                  pl.BlockSpec((None, CMP_HIDDEN, B_SLAB), per_kind)],
        out_specs=pl.BlockSpec((None, None, N_CMP_PAD, B_SLAB), lambda b, k: (k, b, 0, 0)),
        out_shape=jax.ShapeDtypeStruct((2, batch, N_CMP_PAD, B_SLAB), F32),
        compiler_params=_params(("parallel", "arbitrary")),
        name="b_prompt_compress",
    )(p3, p3, w1t, pos, w1, w2wide)


B_TQ = 128
B_TK = 256
B_WIN_SPAN = WIN_B + B_TQ


def _rank_select(s_t, blk, n_rows):
    rank = jnp.zeros(s_t.shape, F32)
    for i in range(n_rows):
        row = s_t[i:i + 1, :]
        ahead = (row > s_t) | ((row == s_t) & (blk > i))
        rank = rank + jnp.where(ahead, 1.0, 0.0)
    return jnp.where(rank < SEL_TOPK, 1.0, 0.0)


def _b_prompt_kernel(slopes_ref, q_ref, gp0_ref, gp1_ref, gp2_ref, gate_ref,
                     kslc_ref, vslc_ref, kwin_ref, vwin_ref, kcmp_ref, vcmp_ref,
                     agg_ref, eg_ref, o_ref,
                     qall_ref, mult_ref, sel_ref, m_ref, l_ref, acc_ref, *, seq):
    qt = pl.program_id(1)
    pos0 = qt * B_TQ
    n_sel = seq // SEL_BLOCK
    lane_g = _lane_group((B_TQ, B_SLAB))
    gmask = [lane_g == g for g in range(B_KV_HEADS)]
    rows_g = B_GROUP * B_TQ

    gates = _sigmoid(gate_ref[...]).astype(BF16)
    for c, gp_ref in enumerate((gp0_ref, gp1_ref, gp2_ref)):
        mult_ref[c] = _dot(gates, eg_ref[:, c * D_MODEL:(c + 1) * D_MODEL]) * _silu(gp_ref[...])

    for r in range(B_GROUP):
        slab = q_ref[:, r * B_SLAB:(r + 1) * B_SLAB] * SCALE
        for g in range(B_KV_HEADS):
            h = g * B_GROUP + r
            qall_ref[h * B_TQ:(h + 1) * B_TQ, :] = jnp.where(gmask[g], slab, 0.0).astype(BF16)

    o_ref[...] = jnp.zeros(o_ref.shape, F32)

    def emit(c, g, res):
        for r in range(B_GROUP):
            part = jnp.where(gmask[g], res[r * B_TQ:(r + 1) * B_TQ], 0.0)
            sl = slice(r * B_SLAB, (r + 1) * B_SLAB)
            o_ref[:, sl] += part * mult_ref[c, :, sl]

    qi = lax.broadcasted_iota(jnp.int32, (B_TQ, N_CMP_PAD), 0)
    ci = lax.broadcasted_iota(jnp.int32, (B_TQ, N_CMP_PAD), 1)

    kcmp = kcmp_ref[...].astype(BF16)
    vcmp = vcmp_ref[...].astype(BF16)
    dist_c = pos0 + qi - (ci * CMP_STRIDE + (CMP_LEN - 1))
    valid_c = dist_c >= 0
    dist_cf = dist_c.astype(F32)
    blk = lax.broadcasted_iota(jnp.int32, (n_sel, B_TQ), 0)
    posq_t = pos0 + lax.broadcasted_iota(jnp.int32, (n_sel, B_TQ), 1)
    cur = posq_t // SEL_BLOCK
    forced = (blk == 0) | (blk == cur) | (blk == cur - 1)
    reachable = blk * SEL_BLOCK <= posq_t
    for g in range(B_KV_HEADS):
        s_all = _dot_nt(qall_ref[g * rows_g:(g + 1) * rows_g, :], kcmp)
        probs = []
        for r in range(B_GROUP):
            s = s_all[r * B_TQ:(r + 1) * B_TQ] - slopes_ref[g, r] * dist_cf
            s = jnp.where(valid_c, s, NEG)
            m = jnp.max(s, axis=-1, keepdims=True)
            e = jnp.where(valid_c, jnp.exp(s - m), 0.0)
            den = jnp.maximum(jnp.sum(e, axis=-1, keepdims=True), 1e-30)
            probs.append(e / den)
        emit(0, g, _dot(jnp.concatenate(probs, axis=0).astype(BF16), vcmp))
        p_sum = probs[0] + probs[1] + probs[2] + probs[3]
        s_blk = _split_dot(p_sum, agg_ref[...])
        s_t = s_blk.T[0:n_sel, :]
        s_t = jnp.where(forced, s_t + FORCE_BONUS, s_t)
        s_t = jnp.where(reachable, s_t, NEG)
        sel_t = _rank_select(s_t, blk, n_sel)
        sel_t = jnp.concatenate([sel_t, jnp.zeros((N_SEL_PAD - n_sel, B_TQ), F32)], axis=0)
        sel_ref[g] = sel_t.T.astype(BF16)

    m_ref[...] = jnp.full(m_ref.shape, NEG, F32)
    l_ref[...] = jnp.zeros(l_ref.shape, F32)
    acc_ref[...] = jnp.zeros(acc_ref.shape, F32)
    qk = lax.broadcasted_iota(jnp.int32, (B_TQ, B_TK), 0)
    ck = lax.broadcasted_iota(jnp.int32, (B_TQ, B_TK), 1)
    eb = lax.broadcasted_iota(jnp.int32, (N_SEL_PAD, B_TK), 0)
    ek = lax.broadcasted_iota(jnp.int32, (N_SEL_PAD, B_TK), 1)

    def chunk(c, carry):
        k0 = pl.multiple_of(c * B_TK, B_TK)
        kc = kslc_ref[pl.ds(k0, B_TK), :].astype(BF16)
        vc = vslc_ref[pl.ds(k0, B_TK), :].astype(BF16)
        dist = pos0 + qk - (k0 + ck)
        causal = dist >= 0
        dist_f = dist.astype(F32)
        expand = jnp.where(eb == (k0 + ek) // SEL_BLOCK, 1.0, 0.0).astype(BF16)
        for g in range(B_KV_HEADS):
            valid = causal & (_dot(sel_ref[g], expand) > 0.5)
            s_all = _dot_nt(qall_ref[g * rows_g:(g + 1) * rows_g, :], kc)
            ps = []
            for r in range(B_GROUP):
                rows = pl.ds((g * B_GROUP + r) * B_TQ, B_TQ)
                s = s_all[r * B_TQ:(r + 1) * B_TQ] - slopes_ref[g, r] * dist_f
                s = jnp.where(valid, s, NEG)
                m_old = m_ref[rows, :]
                m_new = jnp.maximum(m_old, jnp.max(s, axis=-1, keepdims=True))
                alpha = jnp.exp(m_old - m_new)
                p = jnp.exp(s - jnp.concatenate([m_new, m_new], axis=1))
                l_ref[rows, :] = alpha * l_ref[rows, :] + jnp.sum(p, axis=-1, keepdims=True)
                m_ref[rows, :] = m_new
                acc_ref[rows, :] = acc_ref[rows, :] * jnp.concatenate([alpha, alpha], axis=1)
                ps.append(p.astype(BF16))
            rows4 = pl.ds(g * rows_g, rows_g)
            acc_ref[rows4, :] += _dot(jnp.concatenate(ps, axis=0), vc)
        return carry

    lax.fori_loop(0, (pos0 + B_TQ + B_TK - 1) // B_TK, chunk, 0)
    for g in range(B_KV_HEADS):
        rows4 = pl.ds(g * rows_g, rows_g)
        den = jnp.maximum(l_ref[rows4, :], 1e-30)
        emit(1, g, acc_ref[rows4, :] / jnp.concatenate([den, den], axis=1))

    w0 = pl.multiple_of(jnp.maximum(qt - WIN_B // B_TQ, 0) * B_TQ, B_TQ)
    qw = lax.broadcasted_iota(jnp.int32, (B_TQ, B_WIN_SPAN), 0)
    cw = lax.broadcasted_iota(jnp.int32, (B_TQ, B_WIN_SPAN), 1)
    dist_w = pos0 + qw - (w0 + cw)
    valid_w = (dist_w >= 0) & (dist_w <= WIN_B)
    dist_wm = jnp.where(valid_w, dist_w.astype(F32), MASKED_DIST)
    kw = kwin_ref[pl.ds(w0, B_WIN_SPAN), :].astype(BF16)
    vw = vwin_ref[pl.ds(w0, B_WIN_SPAN), :].astype(BF16)
    for g in range(B_KV_HEADS):
        s_all = _dot_nt(qall_ref[g * rows_g:(g + 1) * rows_g, :], kw)
        ps = []
        for r in range(B_GROUP):
            s = s_all[r * B_TQ:(r + 1) * B_TQ] - slopes_ref[g, r] * dist_wm
            m = jnp.max(s, axis=-1, keepdims=True)
            e = jnp.exp(s - m)
            den = jnp.maximum(jnp.sum(e, axis=-1, keepdims=True), 1e-30)
            ps.append((e / den).astype(BF16))
        emit(2, g, _dot(jnp.concatenate(ps, axis=0), vw))


def _b_prompt_attention(p3, cmp_kv, slopes):
    batch, seq, _ = p3.shape
    agg = _sel_aggregation((seq - CMP_LEN) // CMP_STRIDE + 1, seq // SEL_BLOCK)
    eg = _gate_expand_matrix()
    tile = lambda blk: pl.BlockSpec((None, B_TQ, D_MODEL), lambda b, t, blk=blk: (b, t, blk))
    kv = lambda kind: pl.BlockSpec((None, seq, B_SLAB),
                                   lambda b, t, blk=B_KV_COL // B_SLAB + kind: (b, 0, blk))
    fixed = lambda b, t: (0, 0)
    n_rows = N_HEADS * B_TQ
    out = pl.pallas_call(
        functools.partial(_b_prompt_kernel, seq=seq),
        grid=(batch, seq // B_TQ),
        in_specs=[pl.BlockSpec(memory_space=pltpu.SMEM),
                  tile(0), tile(1), tile(2), tile(3),
                  pl.BlockSpec((None, B_TQ, LANES), lambda b, t: (b, t, B_GATE_COL // LANES)),
                  kv(2), kv(3), kv(4), kv(5),
                  pl.BlockSpec((None, None, N_CMP_PAD, B_SLAB), lambda b, t: (0, b, 0, 0)),
                  pl.BlockSpec((None, None, N_CMP_PAD, B_SLAB), lambda b, t: (1, b, 0, 0)),
                  pl.BlockSpec((N_CMP_PAD, N_SEL_PAD), fixed),
                  pl.BlockSpec((LANES, 3 * D_MODEL), fixed)],
        out_specs=pl.BlockSpec((None, B_TQ, D_MODEL), lambda b, t: (b, t, 0)),
        scratch_shapes=[pltpu.VMEM((n_rows, B_SLAB), BF16),
                        pltpu.VMEM((3, B_TQ, D_MODEL), F32),
                        pltpu.VMEM((B_KV_HEADS, B_TQ, N_SEL_PAD), BF16),
                        pltpu.VMEM((n_rows, LANES), F32),
                        pltpu.VMEM((n_rows, LANES), F32),
                        pltpu.VMEM((n_rows, B_SLAB), F32)],
        out_shape=jax.ShapeDtypeStruct((batch, seq, D_MODEL), F32),
        compiler_params=_params(("parallel", "arbitrary")),
        name="b_prompt_attention",
    )(slopes, p3, p3, p3, p3, p3, p3, p3, p3, p3, cmp_kv, cmp_kv, agg, eg)
    return out.reshape(batch * seq, D_MODEL)


PAGE_ROWS = 128
PAGE_COLS = 4 * B_SLAB


def _page_copy(pages_ref, buf_ref, sem_ref, page, slot, p):
    return pltpu.make_async_copy(pages_ref.at[page],
                                 buf_ref.at[slot, pl.ds(p * PAGE_ROWS, PAGE_ROWS)],
                                 sem_ref.at[slot])


def _masked_softmax_rows(s, valid):
    s = jnp.where(valid, s, NEG)
    m = jnp.max(s, axis=-1, keepdims=True)
    e = jnp.where(valid, jnp.exp(s - m), 0.0)
    den = jnp.maximum(jnp.sum(e, axis=-1, keepdims=True), 1e-30)
    return e / den


def _b_sample_kernel(table_ref, pages_ref, q_ref, kvn_ref, gate_ref, gp_ref, win_ref,
                     slope_ref, w1t_ref, pos_ref, w1_ref, w2_ref, agg_ref, expand_ref, gb_ref,
                     o_ref, buf_ref, xs_ref, sem_ref, *, n_pages, page_base):
    i = pl.program_id(0)
    n = pl.num_programs(0)
    slot = i % 2
    n_past = n_pages * PAGE_ROWS
    n_sel = n_past // SEL_BLOCK + 1

    def start_fetch(seq_idx, to_slot):
        for p in range(n_pages):
            _page_copy(pages_ref, buf_ref, sem_ref, page_base + table_ref[seq_idx, p], to_slot, p).start()

    @pl.when(i == 0)
    def _():
        start_fetch(0, 0)

    @pl.when(i + 1 < n)
    def _():
        start_fetch(i + 1, 1 - slot)

    for p in range(n_pages):
        _page_copy(pages_ref, buf_ref, sem_ref, 0, slot, p).wait()

    for cb in range(2 * B_SLAB // LANES):
        xs_ref[cb] = buf_ref[slot, :, cb * LANES:(cb + 1) * LANES]

    slope = slope_ref[:, 0:1]
    row_g = lax.broadcasted_iota(jnp.int32, (N_HEADS, B_SLAB), 0) // B_GROUP
    own = _lane_group((N_HEADS, B_SLAB)) == row_g

    hi = lax.broadcasted_iota(jnp.int32, (N_HEADS, B_GROUP), 0)
    ri = lax.broadcasted_iota(jnp.int32, (N_HEADS, B_GROUP), 1)
    spread = jnp.where(hi % B_GROUP == ri, 1.0, 0.0).astype(BF16)
    q_rows = _dot(spread, (q_ref[...] * SCALE).astype(BF16))
    q_all = jnp.where(own, q_rows, 0.0)
    q_bf = q_all.astype(BF16)

    def attend(keys, values, dist, k_new, v_new, valid):
        s = _dot_nt(q_bf, keys.astype(BF16)) - slope * dist
        s_new = jnp.sum(q_all * k_new, axis=-1, keepdims=True)
        if valid is not None:
            s = jnp.where(valid, s, NEG)
        m = jnp.maximum(jnp.max(s, axis=-1, keepdims=True), s_new)
        e = jnp.exp(s - m)
        if valid is not None:
            e = jnp.where(valid, e, 0.0)
        e_new = jnp.exp(s_new - m)
        den = jnp.maximum(jnp.sum(e, axis=-1, keepdims=True) + e_new, 1e-30)
        return (_dot(e.astype(BF16), values.astype(BF16)) + e_new * v_new) / den

    cmp_kv = []
    for kind in range(2):
        posterm = _dot(pos_ref[kind].astype(BF16), w1_ref[kind])
        load_rows = lambda l, half, kind=kind: xs_ref[2 * kind + half, pl.ds(l, N_CMP_PAD, stride=CMP_STRIDE), :]
        cmp_kv.append(_compress(load_rows, w1t_ref.at[kind], posterm, w2_ref[kind]))
    ci = lax.broadcasted_iota(jnp.int32, (N_HEADS, N_CMP_PAD), 1)
    dist_c = n_past - (ci * CMP_STRIDE + (CMP_LEN - 1))
    valid_c = dist_c >= 0
    s_c = _dot_nt(q_bf, cmp_kv[0].astype(BF16)) - slope * dist_c.astype(F32)
    p_c = _masked_softmax_rows(s_c, valid_c)
    o_cmp = _dot(p_c.astype(BF16), cmp_kv[1].astype(BF16))

    ha = lax.broadcasted_iota(jnp.int32, (N_HEADS, N_HEADS), 0) // B_GROUP
    hb = lax.broadcasted_iota(jnp.int32, (N_HEADS, N_HEADS), 1) // B_GROUP
    group_sum = jnp.where(ha == hb, 1.0, 0.0).astype(BF16)
    p_hi = p_c.astype(BF16)
    p_lo = (p_c - p_hi.astype(F32)).astype(BF16)
    p_grp = _dot(group_sum, p_hi) + _dot(group_sum, p_lo)
    s_blk = _split_dot(p_grp, agg_ref[...])
    blk = lax.broadcasted_iota(jnp.int32, (N_HEADS, N_SEL_PAD), 1)
    cur = n_past // SEL_BLOCK
    forced = (blk == 0) | (blk == cur) | (blk == cur - 1)
    s_blk = jnp.where(forced, s_blk + FORCE_BONUS, s_blk)
    s_blk = jnp.where(blk < n_sel, s_blk, NEG)
    rank = jnp.zeros(s_blk.shape, F32)
    for j in range(n_sel):
        col = s_blk[:, j:j + 1]
        ahead = (col > s_blk) | ((col == s_blk) & (blk > j))
        rank = rank + jnp.where(ahead, 1.0, 0.0)
    sel = jnp.where((rank < SEL_TOPK) & (blk < n_sel), 1.0, 0.0).astype(BF16)
    key_valid = _dot(sel, expand_ref[...]) > 0.5

    kslc = buf_ref[slot, :, 2 * B_SLAB:3 * B_SLAB]
    vslc = buf_ref[slot, :, 3 * B_SLAB:4 * B_SLAB]
    dist_s = (n_past - lax.broadcasted_iota(jnp.int32, (1, n_past), 1)).astype(F32)
    o_sel = attend(kslc, vslc, dist_s, kvn_ref[2:3, :], kvn_ref[3:4, :], key_valid)

    n_buf = win_ref.shape[0]
    dist_w = (n_buf - lax.broadcasted_iota(jnp.int32, (1, n_buf), 1)).astype(F32)
    o_win = attend(win_ref[:, 0:B_SLAB], win_ref[:, B_SLAB:2 * B_SLAB], dist_w,
                   kvn_ref[4:5, :], kvn_ref[5:6, :], None)

    crow = lax.broadcasted_iota(jnp.int32, (3 * B_GROUP, LANES), 0)
    ccol = lax.broadcasted_iota(jnp.int32, (3 * B_GROUP, LANES), 1)
    mine = (ccol // N_HEADS == crow // B_GROUP) & (ccol % B_GROUP == crow % B_GROUP) & (ccol < 3 * N_HEADS)
    gate_rows = jnp.where(mine, _sigmoid(gate_ref[...]), 0.0).astype(BF16)
    mult = _dot(gate_rows, gb_ref[...]) * _silu(gp_ref[...])
    gather = jnp.where(lax.broadcasted_iota(jnp.int32, (B_GROUP, N_HEADS), 1) % B_GROUP
                       == lax.broadcasted_iota(jnp.int32, (B_GROUP, N_HEADS), 0), 1.0, 0.0).astype(BF16)
    out = jnp.zeros((B_GROUP, B_SLAB), F32)
    for c, res in enumerate((o_cmp, o_sel, o_win)):
        res = jnp.where(own, res, 0.0)
        r_hi = res.astype(BF16)
        r_lo = (res - r_hi.astype(F32)).astype(BF16)
        out = out + (_dot(gather, r_hi) + _dot(gather, r_lo)) * mult[c * B_GROUP:(c + 1) * B_GROUP]
    o_ref[...] = out


def _b_sample_attention(proj_s, pages, page_table, win, layer, slopes_rep, cw):
    n = proj_s.shape[0]
    n_pages = page_table.shape[1]
    n_past = n_pages * PAGE_ROWS
    pool = pages.shape[1]
    n_buf = win.shape[2]
    w1t, pos, w1, w2wide = cw
    agg = _sel_aggregation((n_past + 1 - CMP_LEN) // CMP_STRIDE + 1, n_past // SEL_BLOCK + 1)
    expand = (np.arange(N_SEL_PAD)[:, None] == (np.arange(n_past) // SEL_BLOCK)[None, :]).astype(np.float32)
    group_bcast = (np.arange(LANES)[:, None] % N_HEADS // B_GROUP
                   == (np.arange(B_SLAB) // HEAD_DIM)[None, :]).astype(np.float32)
    pages2 = pages.reshape(pages.shape[0] * pool, PAGE_ROWS, PAGE_COLS)
    win2 = win.reshape(win.shape[0] * n, n_buf, 2 * B_SLAB)
    q3 = proj_s[:, B_Q_COL:B_Q_COL + D_MODEL].reshape(n, B_GROUP, B_SLAB)
    gp3 = proj_s[:, B_GPATH_COL:B_GPATH_COL + 3 * D_MODEL].reshape(n, 3 * B_GROUP, B_SLAB)
    kvn = proj_s[:, B_KV_COL:B_KV_COL + 6 * B_SLAB].reshape(n, 6, B_SLAB)
    gate3 = proj_s[:, B_GATE_COL:B_GATE_COL + LANES].reshape(n, 1, LANES)
    per_seq = lambda i, t: (i, 0, 0)
    fixed2 = lambda i, t: (0, 0)
    fixed3 = lambda i, t: (0, 0, 0)
    grid_spec = pltpu.PrefetchScalarGridSpec(
        num_scalar_prefetch=1,
        grid=(n,),
        in_specs=[pl.BlockSpec(memory_space=pl.ANY),
                  pl.BlockSpec((None, B_GROUP, B_SLAB), per_seq),
                  pl.BlockSpec((None, 6, B_SLAB), per_seq),
                  pl.BlockSpec((None, 1, LANES), per_seq),
                  pl.BlockSpec((None, 3 * B_GROUP, B_SLAB), per_seq),
                  pl.BlockSpec((None, n_buf, 2 * B_SLAB), lambda i, t, base=layer * n: (base + i, 0, 0)),
                  pl.BlockSpec((N_HEADS, LANES), fixed2),
                  pl.BlockSpec(w1t.shape, lambda i, t: (0, 0, 0, 0)),
                  pl.BlockSpec(pos.shape, fixed3),
                  pl.BlockSpec(w1.shape, fixed3),
                  pl.BlockSpec(w2wide.shape, fixed3),
                  pl.BlockSpec((N_CMP_PAD, N_SEL_PAD), fixed2),
                  pl.BlockSpec((N_SEL_PAD, n_past), fixed2),
                  pl.BlockSpec((LANES, B_SLAB), fixed2)],
        out_specs=pl.BlockSpec((None, B_GROUP, B_SLAB), per_seq),
        scratch_shapes=[pltpu.VMEM((2, n_past, PAGE_COLS), F32),
                        pltpu.VMEM((2 * B_SLAB // LANES, n_past, LANES), F32),
                        pltpu.SemaphoreType.DMA((2,))],
    )
    out = pl.pallas_call(
        functools.partial(_b_sample_kernel, n_pages=n_pages, page_base=layer * pool),
        grid_spec=grid_spec,
        out_shape=jax.ShapeDtypeStruct((n, B_GROUP, B_SLAB), F32),
        compiler_params=_params(("arbitrary",)),
        name="b_sample_attention",
    )(page_table, pages2, q3, kvn, gate3, gp3, win2, slopes_rep, w1t, pos, w1, w2wide, agg,
      jnp.asarray(expand, BF16), jnp.asarray(group_bcast, BF16))
    return out.reshape(n, D_MODEL)


PROJ_TM = 1024
LN_TM = 512


def _shifted_cache(cache, new_rows):
    return jnp.concatenate([cache[:, :, 1:], new_rows], axis=2)


def kernel(x_prompt, x_sample, cache_a_w128, cache_a_w512, cache_a_w2048, cache_b_pages, cache_b_win,
           page_table, ln_g, ln_b, a_w_in, a_w_out, b_w_in, b_w_out, b_phi_pos, b_phi_w1, b_phi_w2):
    batch, seq, _ = x_prompt.shape
    n = x_sample.shape[0]
    assert x_sample.shape[1] == 1 and seq % (A_TILE * A_GROUPS[-1][1]) == 0
    a_caches = (cache_a_w128, cache_a_w512, cache_a_w2048)
    for (win, dil), c in zip(A_GROUPS, a_caches):
        assert c.shape[2] == win and win == A_CLASS_WINDOW * dil
    assert cache_b_win.shape[2] == WIN_B and seq >= WIN_B

    a_slopes = jnp.asarray(_alibi_slopes(3 * N_HEADS).reshape(N_HEADS, 3).T, F32)
    b_slopes = jnp.asarray(_alibi_slopes(N_HEADS).reshape(B_KV_HEADS, B_GROUP), F32)
    b_slopes_rep = jnp.asarray(np.repeat(_alibi_slopes(N_HEADS)[:, None], LANES, axis=1), F32)
    b_order, head_perm = _b_column_order()

    yp = x_prompt.reshape(batch * seq, D_MODEL)
    ys = x_sample.reshape(n, D_MODEL)
    a_p = [[] for _ in A_GROUPS]
    a_s_new = [[] for _ in A_GROUPS]
    rows_p, rows_s, win_p, win_s_new = [], [], [], []
    for i in range(DEPTH):
        j = i // 2
        if i % 2 == 0:
            w_in = a_w_in[j].astype(BF16)
            w_out = a_w_out[j].astype(BF16)
            proj_p = _matmul(yp, w_in, tm=PROJ_TM, tn=D_MODEL, name=f"a_proj_prompt_{j}")
            proj_s = _matmul(ys, w_in, tm=n, tn=D_MODEL, name=f"a_proj_sample_{j}")
            o_p = _a_prompt_attention(proj_p, a_slopes, batch, seq)
            o_s = _a_sample_attention(proj_s, a_caches, j, a_slopes)
            pp = proj_p.reshape(batch, seq, -1)
            for g, (win, dil) in enumerate(A_GROUPS):
                lo = g * 3 * D_MODEL + D_MODEL
                a_p[g].append(pp[:, seq - min(win, seq):, lo:lo + 2 * D_MODEL]
                              .reshape(batch, min(win, seq), 2, N_HEADS, HEAD_DIM))
                a_s_new[g].append(proj_s[:, lo:lo + 2 * D_MODEL].reshape(n, 1, 2, N_HEADS, HEAD_DIM))
        else:
            w_in_f = b_w_in[j]
            w_in = jnp.where(jnp.asarray(b_order >= 0)[None, :], w_in_f[:, np.maximum(b_order, 0)], 0.0).astype(BF16)
            w_out = b_w_out[j][head_perm, :].astype(BF16)
            proj_p = _matmul(yp, w_in, tm=PROJ_TM, tn=B_COLS // 5, name=f"b_proj_prompt_{j}")
            proj_s = _matmul(ys, w_in, tm=n, tn=B_COLS // 5, name=f"b_proj_sample_{j}")
            p3 = proj_p.reshape(batch, seq, B_COLS)
            cw = _compress_weights(b_phi_pos[j], b_phi_w1[j], b_phi_w2[j])
            cmp_kv = _b_prompt_compress(p3, *cw)
            o_p = _b_prompt_attention(p3, cmp_kv, b_slopes)
            o_s = _b_sample_attention(proj_s, cache_b_pages, page_table, cache_b_win, j, b_slopes_rep, cw)
            rows_p.append(p3[:, :, B_KV_COL:B_KV_COL + 4 * B_SLAB].reshape(batch, seq, 4, B_KV_HEADS, HEAD_DIM))
            win_p.append(p3[:, seq - WIN_B:, B_KV_COL + 4 * B_SLAB:B_KV_COL + 6 * B_SLAB]
                         .reshape(batch, WIN_B, 2, B_KV_HEADS, HEAD_DIM))
            rows_s.append(proj_s[:, B_KV_COL:B_KV_COL + 4 * B_SLAB].reshape(n, 1, 4, B_KV_HEADS, HEAD_DIM))
            win_s_new.append(proj_s[:, B_KV_COL + 4 * B_SLAB:B_KV_COL + 6 * B_SLAB]
                             .reshape(n, 1, 2, B_KV_HEADS, HEAD_DIM))
        yp = _outproj_ln(o_p, yp, w_out, ln_g[i], ln_b[i], tm=LN_TM, name=f"outproj_ln_prompt_{i}")
        ys = _outproj_ln(o_s, ys, w_out, ln_g[i], ln_b[i], tm=n, name=f"outproj_ln_sample_{i}")

    a_s = [_shifted_cache(c, jnp.stack(new)) for c, new in zip(a_caches, a_s_new)]
    return (yp.reshape(batch, seq, D_MODEL), ys.reshape(n, 1, D_MODEL),
            jnp.stack(a_p[0]), a_s[0], jnp.stack(a_p[1]), a_s[1], jnp.stack(a_p[2]), a_s[2],
            jnp.stack(rows_p), jnp.stack(rows_s),
            jnp.stack(win_p), _shifted_cache(cache_b_win, jnp.stack(win_s_new)))
```

```python
import functools

import numpy as np
import jax
import jax.numpy as jnp
from jax import lax
from jax.experimental import pallas as pl
from jax.experimental.pallas import tpu as pltpu

F32 = jnp.float32
BF16 = jnp.bfloat16

D_MODEL = 1024
DEPTH = 4
HEAD_DIM = 64
N_HEADS = D_MODEL // HEAD_DIM
A_GROUPS = ((128, 1), (512, 4), (2048, 16))
A_CLASS_WINDOW = 128
A_QKV_COLS = 3 * 3 * D_MODEL
B_KV_HEADS = 4
B_GROUP = N_HEADS // B_KV_HEADS
B_KV_WIDTH = B_KV_HEADS * HEAD_DIM
CMP_LEN = 32
CMP_STRIDE = 16
CMP_HIDDEN = 4 * HEAD_DIM
SEL_BLOCK = 64
SEL_TOPK = 16
WIN_B = 512
DN_ALPHA = (2 * DEPTH) ** 0.25
LN_EPS = 1e-5
NEG = -1e30
FORCE_BONUS = 1e4
MASKED_DIST = 1e30
SCALE = HEAD_DIM ** -0.5

LANES = 128
VMEM_LIMIT = 56 * 1024 * 1024


def _alibi_slopes(n):
    return 2.0 ** (-8.0 * np.arange(1, n + 1, dtype=np.float32) / n)


def _params(sem, vmem=VMEM_LIMIT):
    return pltpu.CompilerParams(dimension_semantics=sem, vmem_limit_bytes=vmem)


def _dot(a, b):
    return jnp.dot(a, b, preferred_element_type=F32)


def _dot_nt(a, b):
    return lax.dot_general(a, b, (((1,), (1,)), ((), ())), preferred_element_type=F32)


def _silu(x):
    return x / (1.0 + jnp.exp(-x))


def _sigmoid(x):
    return 1.0 / (1.0 + jnp.exp(-x))


def _matmul_kernel(x_ref, w_ref, o_ref):
    o_ref[...] = _dot(x_ref[...].astype(BF16), w_ref[...])


def _matmul(x, w, *, tm, tn, name):
    m, k = x.shape
    n = w.shape[1]
    return pl.pallas_call(
        _matmul_kernel,
        grid=(m // tm, n // tn),
        in_specs=[pl.BlockSpec((tm, k), lambda i, j: (i, 0)),
                  pl.BlockSpec((k, tn), lambda i, j: (0, j))],
        out_specs=pl.BlockSpec((tm, tn), lambda i, j: (i, j)),
        out_shape=jax.ShapeDtypeStruct((m, n), F32),
        compiler_params=_params(("parallel", "arbitrary")),
        name=name,
    )(x, w)


def _outproj_ln_kernel(o_ref, x_ref, w_ref, g_ref, b_ref, y_ref):
    f = _dot(o_ref[...].astype(BF16), w_ref[...])
    z = DN_ALPHA * x_ref[...] + f
    mu = jnp.mean(z, axis=-1, keepdims=True)
    zc = z - mu
    var = jnp.mean(zc * zc, axis=-1, keepdims=True)
    y_ref[...] = zc * lax.rsqrt(var + LN_EPS) * g_ref[...] + b_ref[...]


def _outproj_ln(o, x, w, g, b, *, tm, name):
    m = o.shape[0]
    row = lambda i: (i, 0)
    fixed = lambda i: (0, 0)
    return pl.pallas_call(
        _outproj_ln_kernel,
        grid=(m // tm,),
        in_specs=[pl.BlockSpec((tm, D_MODEL), row), pl.BlockSpec((tm, D_MODEL), row),
                  pl.BlockSpec((D_MODEL, D_MODEL), fixed),
                  pl.BlockSpec((1, D_MODEL), fixed), pl.BlockSpec((1, D_MODEL), fixed)],
        out_specs=pl.BlockSpec((tm, D_MODEL), row),
        out_shape=jax.ShapeDtypeStruct((m, D_MODEL), F32),
        compiler_params=_params(("parallel",)),
        name=name,
    )(o, x, w, g.reshape(1, D_MODEL), b.reshape(1, D_MODEL))


A_TILE = 128
A_UNROLL = 4


def _a_pair_tile(q2, k2, v2, dist_m, sd0, sd1, lane_lo):
    qs = (q2 * SCALE)
    q0 = jnp.where(lane_lo, qs, 0.0).astype(BF16)
    q1 = jnp.where(lane_lo, 0.0, qs).astype(BF16)
    kb = k2.astype(BF16)
    vb = v2.astype(BF16)
    outs, lses = [], []
    for qh, sd in ((q0, sd0), (q1, sd1)):
        s = _dot_nt(qh, kb) - sd * dist_m
        m = jnp.max(s, axis=-1, keepdims=True)
        e = jnp.exp(s - m)
        den = jnp.maximum(jnp.sum(e, axis=-1, keepdims=True), 1e-30)
        outs.append(_dot(e.astype(BF16), vb) / den)
        lses.append(m + jnp.log(den))
    o = jnp.where(lane_lo, outs[0], outs[1])
    lse = jnp.where(lane_lo, lses[0], lses[1])
    return o, lse


def _a_prompt_kernel(slopes_ref, q0_ref, k0_ref, v0_ref, q1_ref, k1_ref, v1_ref,
                     q2_ref, k2_ref, v2_ref, gate_ref, o_ref, acc_ref, lse_ref, *, seq):
    hp = pl.program_id(1)
    lane_lo = lax.broadcasted_iota(jnp.int32, (A_TILE, LANES), 1) < HEAD_DIM
    refs = ((q0_ref, k0_ref, v0_ref), (q1_ref, k1_ref, v1_ref), (q2_ref, k2_ref, v2_ref))
    for g, (win, dil) in enumerate(A_GROUPS):
        q_ref, k_ref, v_ref = refs[g]
        n_cls = seq // dil
        n_tiles = n_cls // A_TILE
        tk = A_TILE if n_tiles == 1 else 2 * A_TILE
        sd0 = slopes_ref[g, 2 * hp] * dil
        sd1 = slopes_ref[g, 2 * hp + 1] * dil
        qi = lax.broadcasted_iota(jnp.int32, (A_TILE, tk), 0)
        ci = lax.broadcasted_iota(jnp.int32, (A_TILE, tk), 1)

        def body(idx, carry, q_ref=q_ref, k_ref=k_ref, v_ref=v_ref, dil=dil, n_tiles=n_tiles,
                 tk=tk, sd0=sd0, sd1=sd1, qi=qi, ci=ci, g=g):
            r = idx // n_tiles
            t = idx % n_tiles
            tprev = jnp.maximum(t - 1, 0)
            q_start = r + dil * A_TILE * t
            k_start = r + dil * A_TILE * tprev if n_tiles > 1 else r
            off = jnp.where(t > 0, A_TILE, 0) if n_tiles > 1 else 0
            d = off + qi - ci
            valid = (d >= 0) & (d <= A_CLASS_WINDOW)
            dist_m = jnp.where(valid, d.astype(F32), MASKED_DIST)
            if dil == 1:
                q2 = q_ref[pl.ds(pl.multiple_of(q_start, A_TILE), A_TILE), :]
                k2 = k_ref[pl.ds(pl.multiple_of(k_start, A_TILE), tk), :]
                v2 = v_ref[pl.ds(pl.multiple_of(k_start, A_TILE), tk), :]
            else:
                q2 = q_ref[pl.ds(q_start, A_TILE, stride=dil), :]
                k2 = k_ref[pl.ds(k_start, tk, stride=dil), :]
                v2 = v_ref[pl.ds(k_start, tk, stride=dil), :]
            o, lse = _a_pair_tile(q2, k2, v2, dist_m, sd0, sd1, lane_lo)
            if dil == 1:
                acc_ref[g, pl.ds(pl.multiple_of(q_start, A_TILE), A_TILE), :] = o
                lse_ref[g, pl.ds(pl.multiple_of(q_start, A_TILE), A_TILE), :] = lse
            else:
                acc_ref[g, pl.ds(q_start, A_TILE, stride=dil), :] = o
                lse_ref[g, pl.ds(q_start, A_TILE, stride=dil), :] = lse
            return carry

        lax.fori_loop(0, seq // A_TILE, body, 0, unroll=A_UNROLL)

    l0, l1, l2 = lse_ref[0], lse_ref[1], lse_ref[2]
    mx = jnp.maximum(jnp.maximum(l0, l1), l2)
    w0, w1, w2 = jnp.exp(l0 - mx), jnp.exp(l1 - mx), jnp.exp(l2 - mx)
    o = (w0 * acc_ref[0] + w1 * acc_ref[1] + w2 * acc_ref[2]) / (w0 + w1 + w2)
    o_ref[...] = o * _silu(gate_ref[...])


def _a_prompt_attention(proj, slopes, batch, seq):
    p3 = proj.reshape(batch, seq, proj.shape[-1])
    n_pairs = D_MODEL // LANES

    def col_spec(col0):
        blk = col0 // LANES
        return pl.BlockSpec((None, seq, LANES), lambda b, hp, blk=blk: (b, 0, blk + hp))

    in_specs = [pl.BlockSpec(memory_space=pltpu.SMEM)]
    for g in range(3):
        for t in range(3):
            in_specs.append(col_spec(g * 3 * D_MODEL + t * D_MODEL))
    in_specs.append(col_spec(A_QKV_COLS))
    out = pl.pallas_call(
        functools.partial(_a_prompt_kernel, seq=seq),
        grid=(batch, n_pairs),
        in_specs=in_specs,
        out_specs=pl.BlockSpec((None, seq, LANES), lambda b, hp: (b, 0, hp)),
        scratch_shapes=[pltpu.VMEM((3, seq, LANES), F32), pltpu.VMEM((3, seq, LANES), F32)],
        out_shape=jax.ShapeDtypeStruct((batch, seq, D_MODEL), F32),
        compiler_params=_params(("parallel", "arbitrary")),
        name="a_prompt_attention",
    )(slopes, *([p3] * 10))
    return out.reshape(batch * seq, D_MODEL)


A_DEC_FEATS = 256
A_DEC_HEADS = A_DEC_FEATS // HEAD_DIM
SUBLANES = 8


def _split_dot(x, w):
    hi = x.astype(BF16)
    lo = (x - hi.astype(F32)).astype(BF16)
    return _dot(hi, w) + _dot(lo, w)


def _lane_column(slab, lane):
    pick = lax.broadcasted_iota(jnp.int32, slab.shape, 1) == lane
    return jnp.sum(jnp.where(pick, slab, 0.0), axis=-1, keepdims=True)


def _advance_rows(xt, new_col):
    rows = xt.shape[1]
    last = lax.broadcasted_iota(jnp.int32, xt.shape, 1) == rows - 1
    return jnp.where(last, new_col, pltpu.roll(xt, rows - 1, 1))


def _a_sample_kernel(slopes_ref, row_ref, colt_ref, c0_ref, c1_ref, c2_ref, *rest, aliased):
    if aliased:
        rest = rest[3:]
    o_ref, n0_ref, n1_ref, n2_ref = rest
    caches = (c0_ref, c1_ref, c2_ref)
    news = (n0_ref, n1_ref, n2_ref)

    @pl.when(pl.program_id(0) == 0)
    def _():
        _a_sample_step(slopes_ref, row_ref, colt_ref, caches, o_ref, news)

    @pl.when(pl.program_id(0) > 0)
    def _():
        for new_ref in news:
            new_ref[...] = jnp.zeros(new_ref.shape, F32)


def _a_sample_step(slopes_ref, row_ref, colt_ref, caches, o_ref, news):
    i = pl.program_id(1)
    hc = pl.program_id(2)
    slabs = D_MODEL // A_DEC_FEATS
    row8 = lax.broadcasted_iota(jnp.int32, (SUBLANES, A_DEC_FEATS), 0)
    own = _lane_group((SUBLANES, A_DEC_FEATS)) == row8
    rcol = lax.broadcasted_iota(jnp.int32, (SUBLANES, 1), 0)
    outs, lses = [], []
    for g, (win, dil) in enumerate(A_GROUPS):
        slab = lambda t, g=g: (g * 3 + t) * slabs + hc
        q = row_ref[pl.ds(slab(0), 1), :] * SCALE
        kn = row_ref[pl.ds(slab(1), 1), :]
        vn = row_ref[pl.ds(slab(2), 1), :]
        kn_col = _lane_column(colt_ref[slab(1)], i)
        vn_col = _lane_column(colt_ref[slab(2)], i)
        kt = caches[g][0]
        vt = caches[g][1]
        rows = kt.shape[1]
        slope = jnp.zeros((SUBLANES, 1), F32)
        for r in range(A_DEC_HEADS):
            slope = jnp.where(rcol == r, slopes_ref[g, hc * A_DEC_HEADS + r], slope)
        qbd = jnp.where(own, jnp.broadcast_to(q, (SUBLANES, A_DEC_FEATS)), 0.0)
        t_idx = lax.broadcasted_iota(jnp.int32, (SUBLANES, rows), 1)
        valid = (t_idx & (dil - 1)) == 0
        dist = (rows - t_idx).astype(F32)
        s = _dot(qbd.astype(BF16), kt.astype(BF16)) - slope * dist
        s = jnp.where(valid, s, NEG)
        s_new = jnp.sum(qbd * kn, axis=-1, keepdims=True)
        m = jnp.maximum(jnp.max(s, axis=-1, keepdims=True), s_new)
        e = jnp.where(valid, jnp.exp(s - m), 0.0)
        e_new = jnp.exp(s_new - m)
        den = jnp.maximum(jnp.sum(e, axis=-1, keepdims=True) + e_new, 1e-30)
        o8 = (_dot_nt(e.astype(BF16), vt.astype(BF16)) + e_new * vn) / den
        outs.append(jnp.sum(jnp.where(own, o8, 0.0), axis=0, keepdims=True))
        lses.append(jnp.sum(jnp.where(own, m + jnp.log(den), 0.0), axis=0, keepdims=True))
        news[g][0] = _advance_rows(kt, kn_col)
        news[g][1] = _advance_rows(vt, vn_col)
    mx = jnp.maximum(jnp.maximum(lses[0], lses[1]), lses[2])
    ws = [jnp.exp(l - mx) for l in lses]
    o = (ws[0] * outs[0] + ws[1] * outs[1] + ws[2] * outs[2]) / (ws[0] + ws[1] + ws[2])
    gate = row_ref[pl.ds(3 * 3 * slabs + hc, 1), :]
    o_ref[...] = o * _silu(gate)


def _feature_major(c):
    layers, n, rows, two, heads, hd = c.shape
    return jnp.transpose(c, (0, 1, 3, 4, 5, 2)).reshape(layers * n, two, heads * hd, rows)


def _row_major(ct, layers, heads):
    ln, two, feats, rows = ct.shape
    c = ct.reshape(layers, ln // layers, two, heads, feats // heads, rows)
    return jnp.transpose(c, (0, 1, 5, 2, 3, 4))


def _a_sample_attention(proj_s, caches_t, prev_t, layer, slopes):
    n = proj_s.shape[0]
    slabs = D_MODEL // A_DEC_FEATS
    n_slabs = proj_s.shape[1] // A_DEC_FEATS
    rows3 = proj_s.reshape(n, n_slabs, A_DEC_FEATS)
    colt = proj_s.T.reshape(n_slabs, A_DEC_FEATS, n)
    layers = caches_t[0].shape[0] // n
    seq_of = lambda p, i: jnp.where(p == 0, i, n - 1)
    slab_of = lambda p, hc: jnp.where(p == 0, hc, slabs - 1)
    cache_in = lambda c: pl.BlockSpec((None, 2, A_DEC_FEATS, c.shape[3]),
                                      lambda p, i, hc: (layer * n + seq_of(p, i), 0, slab_of(p, hc), 0))
    cache_out = lambda c: pl.BlockSpec((None, 2, A_DEC_FEATS, c.shape[3]),
                                       lambda p, i, hc: (((layer + p) % layers) * n + i, 0, hc, 0))
    in_specs = [pl.BlockSpec(memory_space=pltpu.SMEM),
                pl.BlockSpec((None, n_slabs, A_DEC_FEATS), lambda p, i, hc: (seq_of(p, i), 0, 0)),
                pl.BlockSpec((n_slabs, A_DEC_FEATS, n), lambda p, i, hc: (0, 0, 0)),
                *[cache_in(c) for c in caches_t]]
    args = [slopes, rows3, colt, *caches_t]
    aliases = {}
    if prev_t is not None:
        for k, p in enumerate(prev_t):
            aliases[len(args)] = 1 + k
            in_specs.append(pl.BlockSpec(memory_space=pl.ANY))
            args.append(p)
    outs = pl.pallas_call(
        functools.partial(_a_sample_kernel, aliased=prev_t is not None),
        grid=(_fill_phases(prev_t, layers), n, slabs),
        in_specs=in_specs,
        out_specs=[pl.BlockSpec((None, 1, A_DEC_FEATS), lambda p, i, hc: (seq_of(p, i), 0, slab_of(p, hc))),
                   *[cache_out(c) for c in caches_t]],
        out_shape=[jax.ShapeDtypeStruct((n, 1, D_MODEL), F32),
                   *[jax.ShapeDtypeStruct(c.shape, F32) for c in caches_t]],
        input_output_aliases=aliases,
        compiler_params=_params(("arbitrary", "arbitrary", "arbitrary")),
        name="a_sample_attention",
    )(*args)
    return outs[0].reshape(n, D_MODEL), outs[1:]


B_Q_COL = 0
B_GPATH_COL = D_MODEL
B_KV_COL = 4 * D_MODEL
B_GATE_COL = B_KV_COL + 6 * B_KV_WIDTH
B_COLS = B_GATE_COL + LANES
B_SLAB = B_KV_HEADS * HEAD_DIM
N_CMP_PAD = 128
N_SEL_PAD = 128


def _b_column_order():
    head_perm = np.arange(D_MODEL).reshape(B_KV_HEADS, B_GROUP, HEAD_DIM).transpose(1, 0, 2).reshape(-1)
    o_kv = D_MODEL
    o_gate = o_kv + 6 * B_KV_WIDTH
    o_gpath = o_gate + 3 * N_HEADS
    cols = [head_perm]
    for c in range(3):
        cols.append(o_gpath + c * D_MODEL + head_perm)
    cols.append(o_kv + np.arange(6 * B_KV_WIDTH))
    cols.append(o_gate + np.arange(3 * N_HEADS))
    cols.append(np.full((LANES - 3 * N_HEADS,), -1))
    return np.concatenate(cols), head_perm


def _gate_expand_matrix():
    m = np.zeros((LANES, 3 * D_MODEL), np.float32)
    for c in range(3):
        for g in range(B_KV_HEADS):
            for r in range(B_GROUP):
                lo = c * D_MODEL + r * B_SLAB + g * HEAD_DIM
                m[c * N_HEADS + g * B_GROUP + r, lo:lo + HEAD_DIM] = 1.0
    return jnp.asarray(m, BF16)


def _sel_aggregation(n_cmp, n_sel):
    r_sel, r_cmp = SEL_BLOCK // CMP_STRIDE, CMP_LEN // CMP_STRIDE
    agg = np.zeros((N_CMP_PAD, N_SEL_PAD), np.float32)
    for j in range(n_sel):
        for m in range(r_sel):
            for n in range(r_cmp):
                i = r_sel * j + m - n
                if 0 <= i < n_cmp:
                    agg[i, j] += 1.0
    return jnp.asarray(agg, BF16)


def _gelu_tanh(x):
    return 0.5 * x * (1.0 + jnp.tanh(0.7978845608028654 * (x + 0.044715 * (x * x * x))))


def _lane_group(shape):
    return lax.broadcasted_iota(jnp.int32, shape, 1) // HEAD_DIM


def _compress(load_rows, w1t_ref, posterm, w2wide):
    lane_g = _lane_group((N_CMP_PAD, LANES))
    w2_lane_g = _lane_group(w2wide.shape)
    out = jnp.zeros((N_CMP_PAD, B_SLAB), F32)
    for g in range(B_KV_HEADS):
        uv = jnp.zeros((N_CMP_PAD, 2 * CMP_HIDDEN), F32)
        for l in range(CMP_STRIDE):
            xg = jnp.where(lane_g == g % 2, load_rows(l, g // 2), 0.0).astype(BF16)
            uv = uv + _dot(xg, w1t_ref[l])
        nxt = pltpu.roll(uv[:, CMP_HIDDEN:], N_CMP_PAD - 1, 0)
        hid = _gelu_tanh(uv[:, :CMP_HIDDEN] + nxt + posterm)
        out = out + _dot(hid.astype(BF16), jnp.where(w2_lane_g == g, w2wide, 0.0).astype(BF16))
    return out


def _b_compress_kernel(xa_ref, xb_ref, w1t_ref, pos_ref, w1_ref, w2_ref, o_ref):
    posterm = _dot(pos_ref[...].astype(BF16), w1_ref[...])
    halves = (xa_ref, xb_ref)
    load_rows = lambda l, half: halves[half][pl.ds(l, N_CMP_PAD, stride=CMP_STRIDE), :]
    o_ref[...] = _compress(load_rows, w1t_ref, posterm, w2_ref[...])


def _compress_weights(phi_pos, phi_w1, phi_w2):
    w1 = phi_w1.astype(BF16)
    taps = w1.reshape(2, CMP_LEN, HEAD_DIM, CMP_HIDDEN)
    pair = jnp.concatenate([taps[:, :CMP_STRIDE], taps[:, CMP_STRIDE:]], axis=-1)
    w1t = jnp.tile(pair, (1, 1, LANES // HEAD_DIM, 1))
    pos = phi_pos.reshape(2, 1, CMP_LEN * HEAD_DIM)
    w2wide = jnp.tile(phi_w2, (1, 1, B_KV_HEADS))
    return w1t, pos, w1, w2wide


def _b_prompt_compress(p3, w1t, pos, w1, w2wide):
    batch, seq, _ = p3.shape
    kv_blk = B_KV_COL // LANES
    per_kind = lambda b, k: (k, 0, 0)
    half = lambda h: pl.BlockSpec((None, seq, LANES), lambda b, k, h=h: (b, 0, kv_blk + 2 * k + h))
    return pl.pallas_call(
        _b_compress_kernel,
        grid=(batch, 2),
        in_specs=[half(0), half(1),
                  pl.BlockSpec((None, CMP_STRIDE, LANES, 2 * CMP_HIDDEN), lambda b, k: (k, 0, 0, 0)),
                  pl.BlockSpec((None, 1, CMP_LEN * HEAD_DIM), per_kind),
                  pl.BlockSpec((None, CMP_LEN * HEAD_DIM, CMP_HIDDEN), per_kind),
                  pl.BlockSpec((None, CMP_HIDDEN, B_SLAB), per_kind)],
        out_specs=pl.BlockSpec((None, None, N_CMP_PAD, B_SLAB), lambda b, k: (k, b, 0, 0)),
        out_shape=jax.ShapeDtypeStruct((2, batch, N_CMP_PAD, B_SLAB), F32),
        compiler_params=_params(("parallel", "arbitrary")),
        name="b_prompt_compress",
    )(p3, p3, w1t, pos, w1, w2wide)


B_TQ = 128
B_TK = 256
B_WIN_SPAN = WIN_B + B_TQ


def _rank_select(s_t, blk, n_rows):
    rank = jnp.zeros(s_t.shape, F32)
    for i in range(n_rows):
        row = s_t[i:i + 1, :]
        ahead = (row > s_t) | ((row == s_t) & (blk > i))
        rank = rank + jnp.where(ahead, 1.0, 0.0)
    return jnp.where(rank < SEL_TOPK, 1.0, 0.0)


def _b_prompt_kernel(slopes_ref, q_ref, gp0_ref, gp1_ref, gp2_ref, gate_ref,
                     kslc_ref, vslc_ref, kwin_ref, vwin_ref, kcmp_ref, vcmp_ref,
                     agg_ref, eg_ref, o_ref,
                     qall_ref, mult_ref, sel_ref, m_ref, l_ref, acc_ref, *, seq):
    qt = pl.program_id(1)
    pos0 = qt * B_TQ
    n_sel = seq // SEL_BLOCK
    lane_g = _lane_group((B_TQ, B_SLAB))
    gmask = [lane_g == g for g in range(B_KV_HEADS)]
    rows_g = B_GROUP * B_TQ

    gates = _sigmoid(gate_ref[...]).astype(BF16)
    for c, gp_ref in enumerate((gp0_ref, gp1_ref, gp2_ref)):
        mult_ref[c] = _dot(gates, eg_ref[:, c * D_MODEL:(c + 1) * D_MODEL]) * _silu(gp_ref[...])

    for r in range(B_GROUP):
        slab = q_ref[:, r * B_SLAB:(r + 1) * B_SLAB] * SCALE
        for g in range(B_KV_HEADS):
            h = g * B_GROUP + r
            qall_ref[h * B_TQ:(h + 1) * B_TQ, :] = jnp.where(gmask[g], slab, 0.0).astype(BF16)

    o_ref[...] = jnp.zeros(o_ref.shape, F32)

    def emit(c, g, res):
        for r in range(B_GROUP):
            part = jnp.where(gmask[g], res[r * B_TQ:(r + 1) * B_TQ], 0.0)
            sl = slice(r * B_SLAB, (r + 1) * B_SLAB)
            o_ref[:, sl] += part * mult_ref[c, :, sl]

    qi = lax.broadcasted_iota(jnp.int32, (B_TQ, N_CMP_PAD), 0)
    ci = lax.broadcasted_iota(jnp.int32, (B_TQ, N_CMP_PAD), 1)

    kcmp = kcmp_ref[...].astype(BF16)
    vcmp = vcmp_ref[...].astype(BF16)
    dist_c = pos0 + qi - (ci * CMP_STRIDE + (CMP_LEN - 1))
    valid_c = dist_c >= 0
    dist_cf = dist_c.astype(F32)
    blk = lax.broadcasted_iota(jnp.int32, (n_sel, B_TQ), 0)
    posq_t = pos0 + lax.broadcasted_iota(jnp.int32, (n_sel, B_TQ), 1)
    cur = posq_t // SEL_BLOCK
    forced = (blk == 0) | (blk == cur) | (blk == cur - 1)
    reachable = blk * SEL_BLOCK <= posq_t
    for g in range(B_KV_HEADS):
        s_all = _dot_nt(qall_ref[g * rows_g:(g + 1) * rows_g, :], kcmp)
        probs = []
        for r in range(B_GROUP):
            s = s_all[r * B_TQ:(r + 1) * B_TQ] - slopes_ref[g, r] * dist_cf
            s = jnp.where(valid_c, s, NEG)
            m = jnp.max(s, axis=-1, keepdims=True)
            e = jnp.where(valid_c, jnp.exp(s - m), 0.0)
            den = jnp.maximum(jnp.sum(e, axis=-1, keepdims=True), 1e-30)
            probs.append(e / den)
        emit(0, g, _dot(jnp.concatenate(probs, axis=0).astype(BF16), vcmp))
        p_sum = probs[0] + probs[1] + probs[2] + probs[3]
        s_blk = _split_dot(p_sum, agg_ref[...])
        s_t = s_blk.T[0:n_sel, :]
        s_t = jnp.where(forced, s_t + FORCE_BONUS, s_t)
        s_t = jnp.where(reachable, s_t, NEG)
        sel_t = _rank_select(s_t, blk, n_sel)
        sel_t = jnp.concatenate([sel_t, jnp.zeros((N_SEL_PAD - n_sel, B_TQ), F32)], axis=0)
        sel_ref[g] = sel_t.T.astype(BF16)

    m_ref[...] = jnp.full(m_ref.shape, NEG, F32)
    l_ref[...] = jnp.zeros(l_ref.shape, F32)
    acc_ref[...] = jnp.zeros(acc_ref.shape, F32)
    qk = lax.broadcasted_iota(jnp.int32, (B_TQ, B_TK), 0)
    ck = lax.broadcasted_iota(jnp.int32, (B_TQ, B_TK), 1)
    eb = lax.broadcasted_iota(jnp.int32, (N_SEL_PAD, B_TK), 0)
    ek = lax.broadcasted_iota(jnp.int32, (N_SEL_PAD, B_TK), 1)

    def chunk(c, carry):
        k0 = pl.multiple_of(c * B_TK, B_TK)
        kc = kslc_ref[pl.ds(k0, B_TK), :].astype(BF16)
        vc = vslc_ref[pl.ds(k0, B_TK), :].astype(BF16)
        dist = pos0 + qk - (k0 + ck)
        causal = dist >= 0
        dist_f = dist.astype(F32)
        expand = jnp.where(eb == (k0 + ek) // SEL_BLOCK, 1.0, 0.0).astype(BF16)
        for g in range(B_KV_HEADS):
            valid = causal & (_dot(sel_ref[g], expand) > 0.5)
            s_all = _dot_nt(qall_ref[g * rows_g:(g + 1) * rows_g, :], kc)
            ps = []
            for r in range(B_GROUP):
                rows = pl.ds((g * B_GROUP + r) * B_TQ, B_TQ)
                s = s_all[r * B_TQ:(r + 1) * B_TQ] - slopes_ref[g, r] * dist_f
                s = jnp.where(valid, s, NEG)
                m_old = m_ref[rows, :]
                m_new = jnp.maximum(m_old, jnp.max(s, axis=-1, keepdims=True))
                alpha = jnp.exp(m_old - m_new)
                p = jnp.exp(s - jnp.concatenate([m_new, m_new], axis=1))
                l_ref[rows, :] = alpha * l_ref[rows, :] + jnp.sum(p, axis=-1, keepdims=True)
                m_ref[rows, :] = m_new
                acc_ref[rows, :] = acc_ref[rows, :] * jnp.concatenate([alpha, alpha], axis=1)
                ps.append(p.astype(BF16))
            rows4 = pl.ds(g * rows_g, rows_g)
            acc_ref[rows4, :] += _dot(jnp.concatenate(ps, axis=0), vc)
        return carry

    lax.fori_loop(0, (pos0 + B_TQ + B_TK - 1) // B_TK, chunk, 0)
    for g in range(B_KV_HEADS):
        rows4 = pl.ds(g * rows_g, rows_g)
        den = jnp.maximum(l_ref[rows4, :], 1e-30)
        emit(1, g, acc_ref[rows4, :] / jnp.concatenate([den, den], axis=1))

    w0 = pl.multiple_of(jnp.maximum(qt - WIN_B // B_TQ, 0) * B_TQ, B_TQ)
    qw = lax.broadcasted_iota(jnp.int32, (B_TQ, B_WIN_SPAN), 0)
    cw = lax.broadcasted_iota(jnp.int32, (B_TQ, B_WIN_SPAN), 1)
    dist_w = pos0 + qw - (w0 + cw)
    valid_w = (dist_w >= 0) & (dist_w <= WIN_B)
    dist_wm = jnp.where(valid_w, dist_w.astype(F32), MASKED_DIST)
    kw = kwin_ref[pl.ds(w0, B_WIN_SPAN), :].astype(BF16)
    vw = vwin_ref[pl.ds(w0, B_WIN_SPAN), :].astype(BF16)
    for g in range(B_KV_HEADS):
        s_all = _dot_nt(qall_ref[g * rows_g:(g + 1) * rows_g, :], kw)
        ps = []
        for r in range(B_GROUP):
            s = s_all[r * B_TQ:(r + 1) * B_TQ] - slopes_ref[g, r] * dist_wm
            m = jnp.max(s, axis=-1, keepdims=True)
            e = jnp.exp(s - m)
            den = jnp.maximum(jnp.sum(e, axis=-1, keepdims=True), 1e-30)
            ps.append((e / den).astype(BF16))
        emit(2, g, _dot(jnp.concatenate(ps, axis=0), vw))


def _b_prompt_attention(p3, cmp_kv, slopes):
    batch, seq, _ = p3.shape
    agg = _sel_aggregation((seq - CMP_LEN) // CMP_STRIDE + 1, seq // SEL_BLOCK)
    eg = _gate_expand_matrix()
    tile = lambda blk: pl.BlockSpec((None, B_TQ, D_MODEL), lambda b, t, blk=blk: (b, t, blk))
    kv = lambda kind: pl.BlockSpec((None, seq, B_SLAB),
                                   lambda b, t, blk=B_KV_COL // B_SLAB + kind: (b, 0, blk))
    fixed = lambda b, t: (0, 0)
    n_rows = N_HEADS * B_TQ
    out = pl.pallas_call(
        functools.partial(_b_prompt_kernel, seq=seq),
        grid=(batch, seq // B_TQ),
        in_specs=[pl.BlockSpec(memory_space=pltpu.SMEM),
                  tile(0), tile(1), tile(2), tile(3),
                  pl.BlockSpec((None, B_TQ, LANES), lambda b, t: (b, t, B_GATE_COL // LANES)),
                  kv(2), kv(3), kv(4), kv(5),
                  pl.BlockSpec((None, None, N_CMP_PAD, B_SLAB), lambda b, t: (0, b, 0, 0)),
                  pl.BlockSpec((None, None, N_CMP_PAD, B_SLAB), lambda b, t: (1, b, 0, 0)),
                  pl.BlockSpec((N_CMP_PAD, N_SEL_PAD), fixed),
                  pl.BlockSpec((LANES, 3 * D_MODEL), fixed)],
        out_specs=pl.BlockSpec((None, B_TQ, D_MODEL), lambda b, t: (b, t, 0)),
        scratch_shapes=[pltpu.VMEM((n_rows, B_SLAB), BF16),
                        pltpu.VMEM((3, B_TQ, D_MODEL), F32),
                        pltpu.VMEM((B_KV_HEADS, B_TQ, N_SEL_PAD), BF16),
                        pltpu.VMEM((n_rows, LANES), F32),
                        pltpu.VMEM((n_rows, LANES), F32),
                        pltpu.VMEM((n_rows, B_SLAB), F32)],
        out_shape=jax.ShapeDtypeStruct((batch, seq, D_MODEL), F32),
        compiler_params=_params(("parallel", "arbitrary")),
        name="b_prompt_attention",
    )(slopes, p3, p3, p3, p3, p3, p3, p3, p3, p3, cmp_kv, cmp_kv, agg, eg)
    return out.reshape(batch * seq, D_MODEL)


PAGE_ROWS = 128
PAGE_COLS = 4 * B_SLAB


def _page_copy(pages_ref, buf_ref, sem_ref, page, slot, p):
    return pltpu.make_async_copy(pages_ref.at[page],
                                 buf_ref.at[slot, :, pl.ds(p * PAGE_ROWS, PAGE_ROWS)],
                                 sem_ref.at[slot])


def _masked_softmax_rows(s, valid):
    s = jnp.where(valid, s, NEG)
    m = jnp.max(s, axis=-1, keepdims=True)
    e = jnp.where(valid, jnp.exp(s - m), 0.0)
    den = jnp.maximum(jnp.sum(e, axis=-1, keepdims=True), 1e-30)
    return e / den


def _b_sample_kernel(table_ref, pages_ref, q_ref, kvn_ref, gate_ref, gp_ref, win_ref,
                     slope_ref, w1t_ref, pos_ref, w1_ref, w2_ref, agg_ref, expand_ref, gb_ref,
                     o_ref, buf_ref, xs_ref, sem_ref, *, n_pages, page_base):
    i = pl.program_id(0)
    n = pl.num_programs(0)
    slot = i % 2
    n_past = n_pages * PAGE_ROWS
    n_sel = n_past // SEL_BLOCK + 1

    def start_fetch(seq_idx, to_slot):
        for p in range(n_pages):
            _page_copy(pages_ref, buf_ref, sem_ref, page_base + table_ref[seq_idx, p], to_slot, p).start()

    @pl.when(i == 0)
    def _():
        start_fetch(0, 0)

    @pl.when(i + 1 < n)
    def _():
        start_fetch(i + 1, 1 - slot)

    for p in range(n_pages):
        _page_copy(pages_ref, buf_ref, sem_ref, 0, slot, p).wait()

    for cb in range(2 * B_SLAB // LANES):
        xs_ref[cb] = buf_ref[slot, cb * LANES:(cb + 1) * LANES, :].T

    slope = slope_ref[:, 0:1]
    row_g = lax.broadcasted_iota(jnp.int32, (N_HEADS, B_SLAB), 0) // B_GROUP
    own = _lane_group((N_HEADS, B_SLAB)) == row_g

    hi = lax.broadcasted_iota(jnp.int32, (N_HEADS, B_GROUP), 0)
    ri = lax.broadcasted_iota(jnp.int32, (N_HEADS, B_GROUP), 1)
    spread = jnp.where(hi % B_GROUP == ri, 1.0, 0.0).astype(BF16)
    q_rows = _dot(spread, (q_ref[...] * SCALE).astype(BF16))
    q_all = jnp.where(own, q_rows, 0.0)
    q_bf = q_all.astype(BF16)

    def attend(keys_t, values_t, dist, k_new, v_new, valid):
        s = _dot(q_bf, keys_t.astype(BF16)) - slope * dist
        s_new = jnp.sum(q_all * k_new, axis=-1, keepdims=True)
        if valid is not None:
            s = jnp.where(valid, s, NEG)
        m = jnp.maximum(jnp.max(s, axis=-1, keepdims=True), s_new)
        e = jnp.exp(s - m)
        if valid is not None:
            e = jnp.where(valid, e, 0.0)
        e_new = jnp.exp(s_new - m)
        den = jnp.maximum(jnp.sum(e, axis=-1, keepdims=True) + e_new, 1e-30)
        return (_dot_nt(e.astype(BF16), values_t.astype(BF16)) + e_new * v_new) / den

    cmp_kv = []
    for kind in range(2):
        posterm = _dot(pos_ref[kind].astype(BF16), w1_ref[kind])
        load_rows = lambda l, half, kind=kind: xs_ref[2 * kind + half, pl.ds(l, N_CMP_PAD, stride=CMP_STRIDE), :]
        cmp_kv.append(_compress(load_rows, w1t_ref.at[kind], posterm, w2_ref[kind]))
    ci = lax.broadcasted_iota(jnp.int32, (N_HEADS, N_CMP_PAD), 1)
    dist_c = n_past - (ci * CMP_STRIDE + (CMP_LEN - 1))
    valid_c = dist_c >= 0
    s_c = _dot_nt(q_bf, cmp_kv[0].astype(BF16)) - slope * dist_c.astype(F32)
    p_c = _masked_softmax_rows(s_c, valid_c)
    o_cmp = _dot(p_c.astype(BF16), cmp_kv[1].astype(BF16))

    ha = lax.broadcasted_iota(jnp.int32, (N_HEADS, N_HEADS), 0) // B_GROUP
    hb = lax.broadcasted_iota(jnp.int32, (N_HEADS, N_HEADS), 1) // B_GROUP
    group_sum = jnp.where(ha == hb, 1.0, 0.0).astype(BF16)
    p_hi = p_c.astype(BF16)
    p_lo = (p_c - p_hi.astype(F32)).astype(BF16)
    p_grp = _dot(group_sum, p_hi) + _dot(group_sum, p_lo)
    s_blk = _split_dot(p_grp, agg_ref[...])
    blk = lax.broadcasted_iota(jnp.int32, (N_HEADS, N_SEL_PAD), 1)
    cur = n_past // SEL_BLOCK
    forced = (blk == 0) | (blk == cur) | (blk == cur - 1)
    s_blk = jnp.where(forced, s_blk + FORCE_BONUS, s_blk)
    s_blk = jnp.where(blk < n_sel, s_blk, NEG)
    rank = jnp.zeros(s_blk.shape, F32)
    for j in range(n_sel):
        col = s_blk[:, j:j + 1]
        ahead = (col > s_blk) | ((col == s_blk) & (blk > j))
        rank = rank + jnp.where(ahead, 1.0, 0.0)
    sel = jnp.where((rank < SEL_TOPK) & (blk < n_sel), 1.0, 0.0).astype(BF16)
    key_valid = _dot(sel, expand_ref[...]) > 0.5

    kslc_t = buf_ref[slot, 2 * B_SLAB:3 * B_SLAB, :]
    vslc_t = buf_ref[slot, 3 * B_SLAB:4 * B_SLAB, :]
    dist_s = (n_past - lax.broadcasted_iota(jnp.int32, (1, n_past), 1)).astype(F32)
    o_sel = attend(kslc_t, vslc_t, dist_s, kvn_ref[2:3, :], kvn_ref[3:4, :], key_valid)

    kwin_t = win_ref[0]
    vwin_t = win_ref[1]
    n_buf = kwin_t.shape[1]
    dist_w = (n_buf - lax.broadcasted_iota(jnp.int32, (1, n_buf), 1)).astype(F32)
    o_win = attend(kwin_t, vwin_t, dist_w, kvn_ref[4:5, :], kvn_ref[5:6, :], None)

    crow = lax.broadcasted_iota(jnp.int32, (3 * B_GROUP, LANES), 0)
    ccol = lax.broadcasted_iota(jnp.int32, (3 * B_GROUP, LANES), 1)
    mine = (ccol // N_HEADS == crow // B_GROUP) & (ccol % B_GROUP == crow % B_GROUP) & (ccol < 3 * N_HEADS)
    gate_rows = jnp.where(mine, _sigmoid(gate_ref[...]), 0.0).astype(BF16)
    mult = _dot(gate_rows, gb_ref[...]) * _silu(gp_ref[...])
    gather = jnp.where(lax.broadcasted_iota(jnp.int32, (B_GROUP, N_HEADS), 1) % B_GROUP
                       == lax.broadcasted_iota(jnp.int32, (B_GROUP, N_HEADS), 0), 1.0, 0.0).astype(BF16)
    out = jnp.zeros((B_GROUP, B_SLAB), F32)
    for c, res in enumerate((o_cmp, o_sel, o_win)):
        res = jnp.where(own, res, 0.0)
        r_hi = res.astype(BF16)
        r_lo = (res - r_hi.astype(F32)).astype(BF16)
        out = out + (_dot(gather, r_hi) + _dot(gather, r_lo)) * mult[c * B_GROUP:(c + 1) * B_GROUP]
    o_ref[...] = out


def _b_sample_attention(proj_s, pages, page_table, win_t, layer, slopes_rep, cw):
    n = proj_s.shape[0]
    n_pages = page_table.shape[1]
    n_past = n_pages * PAGE_ROWS
    pool = pages.shape[1]
    n_buf = win_t.shape[3]
    w1t, pos, w1, w2wide = cw
    agg = _sel_aggregation((n_past + 1 - CMP_LEN) // CMP_STRIDE + 1, n_past // SEL_BLOCK + 1)
    expand = (np.arange(N_SEL_PAD)[:, None] == (np.arange(n_past) // SEL_BLOCK)[None, :]).astype(np.float32)
    group_bcast = (np.arange(LANES)[:, None] % N_HEADS // B_GROUP
                   == (np.arange(B_SLAB) // HEAD_DIM)[None, :]).astype(np.float32)
    pages_t = jnp.transpose(pages, (0, 1, 3, 4, 5, 2)).reshape(pages.shape[0] * pool, PAGE_COLS, PAGE_ROWS)
    q3 = proj_s[:, B_Q_COL:B_Q_COL + D_MODEL].reshape(n, B_GROUP, B_SLAB)
    gp3 = proj_s[:, B_GPATH_COL:B_GPATH_COL + 3 * D_MODEL].reshape(n, 3 * B_GROUP, B_SLAB)
    kvn = proj_s[:, B_KV_COL:B_KV_COL + 6 * B_SLAB].reshape(n, 6, B_SLAB)
    gate3 = proj_s[:, B_GATE_COL:B_GATE_COL + LANES].reshape(n, 1, LANES)
    per_seq = lambda i, t: (i, 0, 0)
    fixed2 = lambda i, t: (0, 0)
    fixed3 = lambda i, t: (0, 0, 0)
    win_spec = pl.BlockSpec((None, 2, B_SLAB, n_buf), lambda i, t, base=layer * n: (base + i, 0, 0, 0))
    in_specs = [pl.BlockSpec(memory_space=pl.ANY),
                pl.BlockSpec((None, B_GROUP, B_SLAB), per_seq),
                pl.BlockSpec((None, 6, B_SLAB), per_seq),
                pl.BlockSpec((None, 1, LANES), per_seq),
                pl.BlockSpec((None, 3 * B_GROUP, B_SLAB), per_seq),
                win_spec,
                pl.BlockSpec((N_HEADS, LANES), fixed2),
                pl.BlockSpec(w1t.shape, lambda i, t: (0, 0, 0, 0)),
                pl.BlockSpec(pos.shape, fixed3),
                pl.BlockSpec(w1.shape, fixed3),
                pl.BlockSpec(w2wide.shape, fixed3),
                pl.BlockSpec((N_CMP_PAD, N_SEL_PAD), fixed2),
                pl.BlockSpec((N_SEL_PAD, n_past), fixed2),
                pl.BlockSpec((LANES, B_SLAB), fixed2)]
    args = [page_table, pages_t, q3, kvn, gate3, gp3, win_t, slopes_rep, w1t, pos, w1, w2wide, agg,
            jnp.asarray(expand, BF16), jnp.asarray(group_bcast, BF16)]
    grid_spec = pltpu.PrefetchScalarGridSpec(
        num_scalar_prefetch=1,
        grid=(n,),
        in_specs=in_specs,
        out_specs=pl.BlockSpec((None, B_GROUP, B_SLAB), per_seq),
        scratch_shapes=[pltpu.VMEM((2, PAGE_COLS, n_past), F32),
                        pltpu.VMEM((2 * B_SLAB // LANES, n_past, LANES), F32),
                        pltpu.SemaphoreType.DMA((2,))],
    )
    out = pl.pallas_call(
        functools.partial(_b_sample_kernel, n_pages=n_pages, page_base=layer * pool),
        grid_spec=grid_spec,
        out_shape=jax.ShapeDtypeStruct((n, B_GROUP, B_SLAB), F32),
        compiler_params=_params(("arbitrary",)),
        name="b_sample_attention",
    )(*args)
    return out.reshape(n, D_MODEL)


def _advance_window_kernel(win_ref, newt_ref, o_ref):
    i = pl.program_id(1)
    for kv in range(2):
        o_ref[kv] = _advance_rows(win_ref[kv], _lane_column(newt_ref[kv], i))


def _advance_window(win_t, new_t):
    layers, _, feats, n = new_t.shape
    n_buf = win_t.shape[3]
    win_spec = pl.BlockSpec((None, 2, feats, n_buf), lambda l, i: (l * n + i, 0, 0, 0))
    return pl.pallas_call(
        _advance_window_kernel,
        grid=(layers, n),
        in_specs=[win_spec, pl.BlockSpec((None, 2, feats, n), lambda l, i: (l, 0, 0, 0))],
        out_specs=win_spec,
        out_shape=jax.ShapeDtypeStruct(win_t.shape, F32),
        compiler_params=_params(("parallel", "arbitrary")),
        name="b_advance_window",
    )(win_t, new_t)


PROJ_TM = 1024
LN_TM = 512


STATE_ROWS = 512
STATE_FEATS = 256


def _state_kernel(*refs):
    x_ref, o_ref = refs[0], refs[-1]

    @pl.when(pl.program_id(0) == 0)
    def _():
        o_ref[...] = x_ref[...].T

    @pl.when(pl.program_id(0) > 0)
    def _():
        o_ref[...] = jnp.zeros(o_ref.shape, F32)


def _fill_phases(prev, layers):
    return 1 if prev is not None else layers


def _emit_state(proj3, col0, n_feat, n_rows, prev, layer, layers):
    batch, seq, _ = proj3.shape
    tr = min(STATE_ROWS, n_rows)
    r0 = (seq - n_rows) // tr
    c0 = col0 // STATE_FEATS
    n_f, n_t = n_feat // STATE_FEATS, n_rows // tr

    def in_map(p, b, f, t):
        live = p == 0
        return (jnp.where(live, b, batch - 1), r0 + jnp.where(live, t, n_t - 1), c0 + jnp.where(live, f, n_f - 1))

    in_specs = [pl.BlockSpec((None, tr, STATE_FEATS), in_map)]
    args = [proj3]
    aliases = {}
    if prev is not None:
        in_specs.append(pl.BlockSpec(memory_space=pl.ANY))
        args.append(prev)
        aliases = {1: 0}
    return pl.pallas_call(
        _state_kernel,
        grid=(_fill_phases(prev, layers), batch, n_f, n_t),
        in_specs=in_specs,
        out_specs=pl.BlockSpec((None, STATE_FEATS, tr),
                               lambda p, b, f, t: (((layer + p) % layers) * batch + b, f, t)),
        out_shape=jax.ShapeDtypeStruct((layers * batch, n_feat, n_rows), F32),
        input_output_aliases=aliases,
        compiler_params=_params(("arbitrary", "arbitrary", "arbitrary", "arbitrary")),
        name="emit_state",
    )(*args)


def kernel(x_prompt, x_sample, cache_a_w128, cache_a_w512, cache_a_w2048, cache_b_pages, cache_b_win,
           page_table, ln_g, ln_b, a_w_in, a_w_out, b_w_in, b_w_out, b_phi_pos, b_phi_w1, b_phi_w2):
    batch, seq, _ = x_prompt.shape
    n = x_sample.shape[0]
    assert x_sample.shape[1] == 1 and seq % (A_TILE * A_GROUPS[-1][1]) == 0
    a_caches = (cache_a_w128, cache_a_w512, cache_a_w2048)
    for (win, dil), c in zip(A_GROUPS, a_caches):
        assert c.shape[2] == win and win == A_CLASS_WINDOW * dil
    assert cache_b_win.shape[2] == WIN_B and seq >= WIN_B

    a_slopes = jnp.asarray(_alibi_slopes(3 * N_HEADS).reshape(N_HEADS, 3).T, F32)
    b_slopes = jnp.asarray(_alibi_slopes(N_HEADS).reshape(B_KV_HEADS, B_GROUP), F32)
    b_slopes_rep = jnp.asarray(np.repeat(_alibi_slopes(N_HEADS)[:, None], LANES, axis=1), F32)
    b_order, head_perm = _b_column_order()

    yp = x_prompt.reshape(batch * seq, D_MODEL)
    ys = x_sample.reshape(n, D_MODEL)
    n_a = len(a_caches[0])
    n_b = len(cache_b_win)
    a_caches_t = [_feature_major(c) for c in a_caches]
    b_win_t = _feature_major(cache_b_win)
    a_adv_t = None
    a_p_t = [None for _ in A_GROUPS]
    rows_p_t = None
    win_p_t = None
    rows_s, win_new_t = [], []
    for i in range(DEPTH):
        j = i // 2
        if i % 2 == 0:
            w_in = a_w_in[j].astype(BF16)
            w_out = a_w_out[j].astype(BF16)
            proj_p = _matmul(yp, w_in, tm=PROJ_TM, tn=D_MODEL, name=f"a_proj_prompt_{j}")
            proj_s = _matmul(ys, w_in, tm=n, tn=D_MODEL, name=f"a_proj_sample_{j}")
            o_p = _a_prompt_attention(proj_p, a_slopes, batch, seq)
            o_s, a_adv_t = _a_sample_attention(proj_s, a_caches_t, a_adv_t, j, a_slopes)
            pp = proj_p.reshape(batch, seq, -1)
            for g, (win, dil) in enumerate(A_GROUPS):
                a_p_t[g] = _emit_state(pp, g * 3 * D_MODEL + D_MODEL, 2 * D_MODEL, min(win, seq),
                                       a_p_t[g], j, n_a)
        else:
            w_in_f = b_w_in[j]
            w_in = jnp.where(jnp.asarray(b_order >= 0)[None, :], w_in_f[:, np.maximum(b_order, 0)], 0.0).astype(BF16)
            w_out = b_w_out[j][head_perm, :].astype(BF16)
            proj_p = _matmul(yp, w_in, tm=PROJ_TM, tn=B_COLS // 5, name=f"b_proj_prompt_{j}")
            proj_s = _matmul(ys, w_in, tm=n, tn=B_COLS // 5, name=f"b_proj_sample_{j}")
            p3 = proj_p.reshape(batch, seq, B_COLS)
            cw = _compress_weights(b_phi_pos[j], b_phi_w1[j], b_phi_w2[j])
            cmp_kv = _b_prompt_compress(p3, *cw)
            o_p = _b_prompt_attention(p3, cmp_kv, b_slopes)
            o_s = _b_sample_attention(proj_s, cache_b_pages, page_table, b_win_t, j, b_slopes_rep, cw)
            rows_p_t = _emit_state(p3, B_KV_COL, 4 * B_SLAB, seq, rows_p_t, j, n_b)
            win_p_t = _emit_state(p3, B_KV_COL + 4 * B_SLAB, 2 * B_SLAB, WIN_B, win_p_t, j, n_b)
            rows_s.append(proj_s[:, B_KV_COL:B_KV_COL + 4 * B_SLAB].reshape(n, 1, 4, B_KV_HEADS, HEAD_DIM))
            win_new_t.append(proj_s[:, B_KV_COL + 4 * B_SLAB:B_KV_COL + 6 * B_SLAB].T.reshape(2, B_SLAB, n))
        yp = _outproj_ln(o_p, yp, w_out, ln_g[i], ln_b[i], tm=LN_TM, name=f"outproj_ln_prompt_{i}")
        ys = _outproj_ln(o_s, ys, w_out, ln_g[i], ln_b[i], tm=n, name=f"outproj_ln_sample_{i}")

    def states(ct, kinds, layers, heads):
        ln, feats, rows = ct.shape
        return _row_major(ct.reshape(ln, kinds, feats // kinds, rows), layers, heads)

    a_s = [_row_major(ct, n_a, N_HEADS) for ct in a_adv_t]
    a_p = [states(ct, 2, n_a, N_HEADS) for ct in a_p_t]
    return (yp.reshape(batch, seq, D_MODEL), ys.reshape(n, 1, D_MODEL),
            a_p[0], a_s[0], a_p[1], a_s[1], a_p[2], a_s[2],
            states(rows_p_t, 4, n_b, B_KV_HEADS), jnp.stack(rows_s),
            states(win_p_t, 2, n_b, B_KV_HEADS),
            _row_major(_advance_window(b_win_t, jnp.stack(win_new_t)), n_b, B_KV_HEADS))
```

```python
import functools

import numpy as np
import jax
import jax.numpy as jnp
from jax import lax
from jax.experimental import pallas as pl
from jax.experimental.pallas import tpu as pltpu

F32 = jnp.float32
BF16 = jnp.bfloat16

D_MODEL = 1024
DEPTH = 4
HEAD_DIM = 64
N_HEADS = D_MODEL // HEAD_DIM
A_GROUPS = ((128, 1), (512, 4), (2048, 16))
A_CLASS_WINDOW = 128
A_QKV_COLS = 3 * 3 * D_MODEL
B_KV_HEADS = 4
B_GROUP = N_HEADS // B_KV_HEADS
B_KV_WIDTH = B_KV_HEADS * HEAD_DIM
CMP_LEN = 32
CMP_STRIDE = 16
CMP_HIDDEN = 4 * HEAD_DIM
SEL_BLOCK = 64
SEL_TOPK = 16
WIN_B = 512
DN_ALPHA = (2 * DEPTH) ** 0.25
LN_EPS = 1e-5
NEG = -1e30
FORCE_BONUS = 1e4
MASKED_DIST = 1e30
SCALE = HEAD_DIM ** -0.5

LANES = 128
VMEM_LIMIT = 56 * 1024 * 1024


def _alibi_slopes(n):
    return 2.0 ** (-8.0 * np.arange(1, n + 1, dtype=np.float32) / n)


def _params(sem, vmem=VMEM_LIMIT):
    return pltpu.CompilerParams(dimension_semantics=sem, vmem_limit_bytes=vmem)


def _dot(a, b):
    return jnp.dot(a, b, preferred_element_type=F32)


def _dot_nt(a, b):
    return lax.dot_general(a, b, (((1,), (1,)), ((), ())), preferred_element_type=F32)


def _silu(x):
    return x / (1.0 + jnp.exp(-x))


def _sigmoid(x):
    return 1.0 / (1.0 + jnp.exp(-x))


def _matmul_kernel(x_ref, w_ref, o_ref):
    o_ref[...] = _dot(x_ref[...].astype(BF16), w_ref[...])


def _matmul(x, w, *, tm, tn, name):
    m, k = x.shape
    n = w.shape[1]
    return pl.pallas_call(
        _matmul_kernel,
        grid=(m // tm, n // tn),
        in_specs=[pl.BlockSpec((tm, k), lambda i, j: (i, 0)),
                  pl.BlockSpec((k, tn), lambda i, j: (0, j))],
        out_specs=pl.BlockSpec((tm, tn), lambda i, j: (i, j)),
        out_shape=jax.ShapeDtypeStruct((m, n), F32),
        compiler_params=_params(("parallel", "arbitrary")),
        name=name,
    )(x, w)


def _outproj_ln_kernel(o_ref, x_ref, w_ref, g_ref, b_ref, y_ref):
    f = _dot(o_ref[...].astype(BF16), w_ref[...])
    z = DN_ALPHA * x_ref[...] + f
    mu = jnp.mean(z, axis=-1, keepdims=True)
    zc = z - mu
    var = jnp.mean(zc * zc, axis=-1, keepdims=True)
    y_ref[...] = zc * lax.rsqrt(var + LN_EPS) * g_ref[...] + b_ref[...]


def _outproj_ln(o, x, w, g, b, *, tm, name):
    m = o.shape[0]
    row = lambda i: (i, 0)
    fixed = lambda i: (0, 0)
    return pl.pallas_call(
        _outproj_ln_kernel,
        grid=(m // tm,),
        in_specs=[pl.BlockSpec((tm, D_MODEL), row), pl.BlockSpec((tm, D_MODEL), row),
                  pl.BlockSpec((D_MODEL, D_MODEL), fixed),
                  pl.BlockSpec((1, D_MODEL), fixed), pl.BlockSpec((1, D_MODEL), fixed)],
        out_specs=pl.BlockSpec((tm, D_MODEL), row),
        out_shape=jax.ShapeDtypeStruct((m, D_MODEL), F32),
        compiler_params=_params(("parallel",)),
        name=name,
    )(o, x, w, g.reshape(1, D_MODEL), b.reshape(1, D_MODEL))


A_TILE = 128
A_UNROLL = 8


def _a_pair_tile(q2, k2, v2, dist_m, sd0, sd1, lane_lo):
    qs = (q2 * SCALE)
    q0 = jnp.where(lane_lo, qs, 0.0).astype(BF16)
    q1 = jnp.where(lane_lo, 0.0, qs).astype(BF16)
    kb = k2.astype(BF16)
    vb = v2.astype(BF16)
    outs, lses = [], []
    for qh, sd in ((q0, sd0), (q1, sd1)):
        s = _dot_nt(qh, kb) - sd * dist_m
        m = jnp.max(s, axis=-1, keepdims=True)
        e = jnp.exp(s - m)
        den = jnp.maximum(jnp.sum(e, axis=-1, keepdims=True), 1e-30)
        outs.append(_dot(e.astype(BF16), vb) / den)
        lses.append(m + jnp.log(den))
    o = jnp.where(lane_lo, outs[0], outs[1])
    lse = jnp.where(lane_lo, lses[0], lses[1])
    return o, lse


def _a_prompt_kernel(slopes_ref, q0_ref, k0_ref, v0_ref, q1_ref, k1_ref, v1_ref,
                     q2_ref, k2_ref, v2_ref, gate_ref, o_ref, acc_ref, lse_ref, *, seq):
    hp = pl.program_id(1)
    lane_lo = lax.broadcasted_iota(jnp.int32, (A_TILE, LANES), 1) < HEAD_DIM
    refs = ((q0_ref, k0_ref, v0_ref), (q1_ref, k1_ref, v1_ref), (q2_ref, k2_ref, v2_ref))
    for g, (win, dil) in enumerate(A_GROUPS):
        q_ref, k_ref, v_ref = refs[g]
        n_cls = seq // dil
        n_tiles = n_cls // A_TILE
        tk = A_TILE if n_tiles == 1 else 2 * A_TILE
        sd0 = slopes_ref[g, 2 * hp] * dil
        sd1 = slopes_ref[g, 2 * hp + 1] * dil
        qi = lax.broadcasted_iota(jnp.int32, (A_TILE, tk), 0)
        ci = lax.broadcasted_iota(jnp.int32, (A_TILE, tk), 1)

        def body(idx, carry, q_ref=q_ref, k_ref=k_ref, v_ref=v_ref, dil=dil, n_tiles=n_tiles,
                 tk=tk, sd0=sd0, sd1=sd1, qi=qi, ci=ci, g=g):
            r = idx // n_tiles
            t = idx % n_tiles
            tprev = jnp.maximum(t - 1, 0)
            q_start = r + dil * A_TILE * t
            k_start = r + dil * A_TILE * tprev if n_tiles > 1 else r
            off = jnp.where(t > 0, A_TILE, 0) if n_tiles > 1 else 0
            d = off + qi - ci
            valid = (d >= 0) & (d <= A_CLASS_WINDOW)
            dist_m = jnp.where(valid, d.astype(F32), MASKED_DIST)
            if dil == 1:
                q2 = q_ref[pl.ds(pl.multiple_of(q_start, A_TILE), A_TILE), :]
                k2 = k_ref[pl.ds(pl.multiple_of(k_start, A_TILE), tk), :]
                v2 = v_ref[pl.ds(pl.multiple_of(k_start, A_TILE), tk), :]
            else:
                q2 = q_ref[pl.ds(q_start, A_TILE, stride=dil), :]
                k2 = k_ref[pl.ds(k_start, tk, stride=dil), :]
                v2 = v_ref[pl.ds(k_start, tk, stride=dil), :]
            o, lse = _a_pair_tile(q2, k2, v2, dist_m, sd0, sd1, lane_lo)
            if dil == 1:
                acc_ref[g, pl.ds(pl.multiple_of(q_start, A_TILE), A_TILE), :] = o
                lse_ref[g, pl.ds(pl.multiple_of(q_start, A_TILE), A_TILE), :] = lse
            else:
                acc_ref[g, pl.ds(q_start, A_TILE, stride=dil), :] = o
                lse_ref[g, pl.ds(q_start, A_TILE, stride=dil), :] = lse
            return carry

        lax.fori_loop(0, seq // A_TILE, body, 0, unroll=A_UNROLL)

    l0, l1, l2 = lse_ref[0], lse_ref[1], lse_ref[2]
    mx = jnp.maximum(jnp.maximum(l0, l1), l2)
    w0, w1, w2 = jnp.exp(l0 - mx), jnp.exp(l1 - mx), jnp.exp(l2 - mx)
    o = (w0 * acc_ref[0] + w1 * acc_ref[1] + w2 * acc_ref[2]) / (w0 + w1 + w2)
    o_ref[...] = o * _silu(gate_ref[...])


def _a_prompt_attention(proj, slopes, batch, seq):
    p3 = proj.reshape(batch, seq, proj.shape[-1])
    n_pairs = D_MODEL // LANES

    def col_spec(col0):
        blk = col0 // LANES
        return pl.BlockSpec((None, seq, LANES), lambda b, hp, blk=blk: (b, 0, blk + hp))

    in_specs = [pl.BlockSpec(memory_space=pltpu.SMEM)]
    for g in range(3):
        for t in range(3):
            in_specs.append(col_spec(g * 3 * D_MODEL + t * D_MODEL))
    in_specs.append(col_spec(A_QKV_COLS))
    out = pl.pallas_call(
        functools.partial(_a_prompt_kernel, seq=seq),
        grid=(batch, n_pairs),
        in_specs=in_specs,
        out_specs=pl.BlockSpec((None, seq, LANES), lambda b, hp: (b, 0, hp)),
        scratch_shapes=[pltpu.VMEM((3, seq, LANES), F32), pltpu.VMEM((3, seq, LANES), F32)],
        out_shape=jax.ShapeDtypeStruct((batch, seq, D_MODEL), F32),
        compiler_params=_params(("parallel", "arbitrary")),
        name="a_prompt_attention",
    )(slopes, *([p3] * 10))
    return out.reshape(batch * seq, D_MODEL)


A_DEC_FEATS = 256
A_DEC_HEADS = A_DEC_FEATS // HEAD_DIM
SUBLANES = 8


def _split_dot(x, w):
    hi = x.astype(BF16)
    lo = (x - hi.astype(F32)).astype(BF16)
    return _dot(hi, w) + _dot(lo, w)


def _lane_column(slab, lane):
    pick = lax.broadcasted_iota(jnp.int32, slab.shape, 1) == lane
    return jnp.sum(jnp.where(pick, slab, 0.0), axis=-1, keepdims=True)


def _advance_rows(xt, new_col):
    rows = xt.shape[1]
    last = lax.broadcasted_iota(jnp.int32, xt.shape, 1) == rows - 1
    return jnp.where(last, new_col, pltpu.roll(xt, rows - 1, 1))


def _a_sample_kernel(slopes_ref, row_ref, colt_ref, c0_ref, c1_ref, c2_ref, *rest, aliased):
    if aliased:
        rest = rest[3:]
    o_ref, n0_ref, n1_ref, n2_ref = rest
    caches = (c0_ref, c1_ref, c2_ref)
    news = (n0_ref, n1_ref, n2_ref)

    @pl.when(pl.program_id(0) == 0)
    def _():
        _a_sample_step(slopes_ref, row_ref, colt_ref, caches, o_ref, news)

    @pl.when(pl.program_id(0) > 0)
    def _():
        for new_ref in news:
            new_ref[...] = jnp.zeros(new_ref.shape, F32)


def _a_sample_step(slopes_ref, row_ref, colt_ref, caches, o_ref, news):
    i = pl.program_id(1)
    hc = pl.program_id(2)
    slabs = D_MODEL // A_DEC_FEATS
    row8 = lax.broadcasted_iota(jnp.int32, (SUBLANES, A_DEC_FEATS), 0)
    own = _lane_group((SUBLANES, A_DEC_FEATS)) == row8
    rcol = lax.broadcasted_iota(jnp.int32, (SUBLANES, 1), 0)
    outs, lses = [], []
    for g, (win, dil) in enumerate(A_GROUPS):
        slab = lambda t, g=g: (g * 3 + t) * slabs + hc
        q = row_ref[pl.ds(slab(0), 1), :] * SCALE
        kn = row_ref[pl.ds(slab(1), 1), :]
        vn = row_ref[pl.ds(slab(2), 1), :]
        kn_col = _lane_column(colt_ref[slab(1)], i)
        vn_col = _lane_column(colt_ref[slab(2)], i)
        kt = caches[g][0]
        vt = caches[g][1]
        rows = kt.shape[1]
        slope = jnp.zeros((SUBLANES, 1), F32)
        for r in range(A_DEC_HEADS):
            slope = jnp.where(rcol == r, slopes_ref[g, hc * A_DEC_HEADS + r], slope)
        qbd = jnp.where(own, jnp.broadcast_to(q, (SUBLANES, A_DEC_FEATS)), 0.0)
        t_idx = lax.broadcasted_iota(jnp.int32, (SUBLANES, rows), 1)
        valid = (t_idx & (dil - 1)) == 0
        dist = (rows - t_idx).astype(F32)
        s = _dot(qbd.astype(BF16), kt.astype(BF16)) - slope * dist
        s = jnp.where(valid, s, NEG)
        s_new = jnp.sum(qbd * kn, axis=-1, keepdims=True)
        m = jnp.maximum(jnp.max(s, axis=-1, keepdims=True), s_new)
        e = jnp.where(valid, jnp.exp(s - m), 0.0)
        e_new = jnp.exp(s_new - m)
        den = jnp.maximum(jnp.sum(e, axis=-1, keepdims=True) + e_new, 1e-30)
        o8 = (_dot_nt(e.astype(BF16), vt.astype(BF16)) + e_new * vn) / den
        outs.append(jnp.sum(jnp.where(own, o8, 0.0), axis=0, keepdims=True))
        lses.append(jnp.sum(jnp.where(own, m + jnp.log(den), 0.0), axis=0, keepdims=True))
        news[g][0] = _advance_rows(kt, kn_col)
        news[g][1] = _advance_rows(vt, vn_col)
    mx = jnp.maximum(jnp.maximum(lses[0], lses[1]), lses[2])
    ws = [jnp.exp(l - mx) for l in lses]
    o = (ws[0] * outs[0] + ws[1] * outs[1] + ws[2] * outs[2]) / (ws[0] + ws[1] + ws[2])
    gate = row_ref[pl.ds(3 * 3 * slabs + hc, 1), :]
    o_ref[...] = o * _silu(gate)


def _feature_major(c):
    layers, n, rows, two, heads, hd = c.shape
    return jnp.transpose(c, (0, 1, 3, 4, 5, 2)).reshape(layers * n, two, heads * hd, rows)


def _row_major(ct, layers, heads):
    ln, two, feats, rows = ct.shape
    c = ct.reshape(layers, ln // layers, two, heads, feats // heads, rows)
    return jnp.transpose(c, (0, 1, 5, 2, 3, 4))


def _a_sample_attention(proj_s, caches_t, prev_t, layer, slopes):
    n = proj_s.shape[0]
    slabs = D_MODEL // A_DEC_FEATS
    n_slabs = proj_s.shape[1] // A_DEC_FEATS
    rows3 = proj_s.reshape(n, n_slabs, A_DEC_FEATS)
    colt = proj_s.T.reshape(n_slabs, A_DEC_FEATS, n)
    layers = caches_t[0].shape[0] // n
    seq_of = lambda p, i: jnp.where(p == 0, i, n - 1)
    slab_of = lambda p, hc: jnp.where(p == 0, hc, slabs - 1)
    cache_in = lambda c: pl.BlockSpec((None, 2, A_DEC_FEATS, c.shape[3]),
                                      lambda p, i, hc: (layer * n + seq_of(p, i), 0, slab_of(p, hc), 0))
    cache_out = lambda c: pl.BlockSpec((None, 2, A_DEC_FEATS, c.shape[3]),
                                       lambda p, i, hc: (((layer + p) % layers) * n + i, 0, hc, 0))
    in_specs = [pl.BlockSpec(memory_space=pltpu.SMEM),
                pl.BlockSpec((None, n_slabs, A_DEC_FEATS), lambda p, i, hc: (seq_of(p, i), 0, 0)),
                pl.BlockSpec((n_slabs, A_DEC_FEATS, n), lambda p, i, hc: (0, 0, 0)),
                *[cache_in(c) for c in caches_t]]
    args = [slopes, rows3, colt, *caches_t]
    aliases = {}
    if prev_t is not None:
        for k, p in enumerate(prev_t):
            aliases[len(args)] = 1 + k
            in_specs.append(pl.BlockSpec(memory_space=pl.ANY))
            args.append(p)
    outs = pl.pallas_call(
        functools.partial(_a_sample_kernel, aliased=prev_t is not None),
        grid=(_fill_phases(prev_t, layers), n, slabs),
        in_specs=in_specs,
        out_specs=[pl.BlockSpec((None, 1, A_DEC_FEATS), lambda p, i, hc: (seq_of(p, i), 0, slab_of(p, hc))),
                   *[cache_out(c) for c in caches_t]],
        out_shape=[jax.ShapeDtypeStruct((n, 1, D_MODEL), F32),
                   *[jax.ShapeDtypeStruct(c.shape, F32) for c in caches_t]],
        input_output_aliases=aliases,
        compiler_params=_params(("arbitrary", "arbitrary", "arbitrary")),
        name="a_sample_attention",
    )(*args)
    return outs[0].reshape(n, D_MODEL), outs[1:]


B_Q_COL = 0
B_GPATH_COL = D_MODEL
B_KV_COL = 4 * D_MODEL
B_GATE_COL = B_KV_COL + 6 * B_KV_WIDTH
B_COLS = B_GATE_COL + LANES
B_SLAB = B_KV_HEADS * HEAD_DIM
N_CMP_PAD = 128
N_SEL_PAD = 128


def _b_column_order():
    head_perm = np.arange(D_MODEL).reshape(B_KV_HEADS, B_GROUP, HEAD_DIM).transpose(1, 0, 2).reshape(-1)
    o_kv = D_MODEL
    o_gate = o_kv + 6 * B_KV_WIDTH
    o_gpath = o_gate + 3 * N_HEADS
    cols = [head_perm]
    for c in range(3):
        cols.append(o_gpath + c * D_MODEL + head_perm)
    cols.append(o_kv + np.arange(6 * B_KV_WIDTH))
    cols.append(o_gate + np.arange(3 * N_HEADS))
    cols.append(np.full((LANES - 3 * N_HEADS,), -1))
    return np.concatenate(cols), head_perm


def _gate_expand_matrix():
    m = np.zeros((LANES, 3 * D_MODEL), np.float32)
    for c in range(3):
        for g in range(B_KV_HEADS):
            for r in range(B_GROUP):
                lo = c * D_MODEL + r * B_SLAB + g * HEAD_DIM
                m[c * N_HEADS + g * B_GROUP + r, lo:lo + HEAD_DIM] = 1.0
    return jnp.asarray(m, BF16)


def _sel_aggregation(n_cmp, n_sel):
    r_sel, r_cmp = SEL_BLOCK // CMP_STRIDE, CMP_LEN // CMP_STRIDE
    agg = np.zeros((N_CMP_PAD, N_SEL_PAD), np.float32)
    for j in range(n_sel):
        for m in range(r_sel):
            for n in range(r_cmp):
                i = r_sel * j + m - n
                if 0 <= i < n_cmp:
                    agg[i, j] += 1.0
    return jnp.asarray(agg, BF16)


def _gelu_tanh(x):
    return 0.5 * x * (1.0 + jnp.tanh(0.7978845608028654 * (x + 0.044715 * (x * x * x))))


def _lane_group(shape):
    return lax.broadcasted_iota(jnp.int32, shape, 1) // HEAD_DIM


def _compress(load_rows, w1t_ref, posterm, w2wide):
    w2_lane_g = _lane_group(w2wide.shape)
    out = jnp.zeros((N_CMP_PAD, B_SLAB), F32)
    half_taps = CMP_STRIDE // 2
    for half in range(B_KV_HEADS // 2):
        uv2 = jnp.zeros((N_CMP_PAD, 4 * CMP_HIDDEN), F32)
        for lp in range(half_taps):
            x2 = jnp.concatenate([load_rows(lp, half), load_rows(lp + half_taps, half)], axis=1)
            uv2 = uv2 + _dot(x2.astype(BF16), w1t_ref[lp])
        for sub in range(2):
            g = 2 * half + sub
            uv = uv2[:, sub * 2 * CMP_HIDDEN:(sub + 1) * 2 * CMP_HIDDEN]
            nxt = pltpu.roll(uv[:, CMP_HIDDEN:], N_CMP_PAD - 1, 0)
            hid = _gelu_tanh(uv[:, :CMP_HIDDEN] + nxt + posterm)
            out = out + _dot(hid.astype(BF16), jnp.where(w2_lane_g == g, w2wide, 0.0).astype(BF16))
    return out


def _b_compress_kernel(xa_ref, xb_ref, w1t_ref, pos_ref, w1_ref, w2_ref, o_ref):
    posterm = _dot(pos_ref[...].astype(BF16), w1_ref[...])
    halves = (xa_ref, xb_ref)
    load_rows = lambda l, half: halves[half][pl.ds(l, N_CMP_PAD, stride=CMP_STRIDE), :]
    o_ref[...] = _compress(load_rows, w1t_ref, posterm, w2_ref[...])


def _compress_weights(phi_pos, phi_w1, phi_w2):
    w1 = phi_w1.astype(BF16)
    taps = w1.reshape(2, CMP_LEN, HEAD_DIM, CMP_HIDDEN)
    pair = jnp.concatenate([taps[:, :CMP_STRIDE], taps[:, CMP_STRIDE:]], axis=-1)
    zero = jnp.zeros_like(pair)
    diag = jnp.concatenate([jnp.concatenate([pair, zero], axis=-1),
                            jnp.concatenate([zero, pair], axis=-1)], axis=-2)
    w1t = jnp.concatenate([diag[:, :CMP_STRIDE // 2], diag[:, CMP_STRIDE // 2:]], axis=-2)
    pos = phi_pos.reshape(2, 1, CMP_LEN * HEAD_DIM)
    w2wide = jnp.tile(phi_w2, (1, 1, B_KV_HEADS))
    return w1t, pos, w1, w2wide


def _b_prompt_compress(p3, w1t, pos, w1, w2wide):
    batch, seq, _ = p3.shape
    kv_blk = B_KV_COL // LANES
    per_kind = lambda b, k: (k, 0, 0)
    half = lambda h: pl.BlockSpec((None, seq, LANES), lambda b, k, h=h: (b, 0, kv_blk + 2 * k + h))
    return pl.pallas_call(
        _b_compress_kernel,
        grid=(batch, 2),
        in_specs=[half(0), half(1),
                  pl.BlockSpec((None,) + w1t.shape[1:], lambda b, k: (k, 0, 0, 0)),
                  pl.BlockSpec((None, 1, CMP_LEN * HEAD_DIM), per_kind),
                  pl.BlockSpec((None, CMP_LEN * HEAD_DIM, CMP_HIDDEN), per_kind),
                  pl.BlockSpec((None, CMP_HIDDEN, B_SLAB), per_kind)],
        out_specs=pl.BlockSpec((None, None, N_CMP_PAD, B_SLAB), lambda b, k: (k, b, 0, 0)),
        out_shape=jax.ShapeDtypeStruct((2, batch, N_CMP_PAD, B_SLAB), F32),
        compiler_params=_params(("parallel", "arbitrary")),
        name="b_prompt_compress",
    )(p3, p3, w1t, pos, w1, w2wide)


B_TQ = 128
B_TK = 256
B_WIN_SPAN = WIN_B + B_TQ


def _rank_select(s_t, blk, n_rows):
    rank = jnp.zeros(s_t.shape, F32)
    for i in range(n_rows):
        row = s_t[i:i + 1, :]
        ahead = (row > s_t) | ((row == s_t) & (blk > i))
        rank = rank + jnp.where(ahead, 1.0, 0.0)
    return jnp.where(rank < SEL_TOPK, 1.0, 0.0)


def _b_prompt_kernel(slopes_ref, q_ref, gp0_ref, gp1_ref, gp2_ref, gate_ref,
                     kslc_ref, vslc_ref, kwin_ref, vwin_ref, kcmp_ref, vcmp_ref,
                     agg_ref, eg_ref, o_ref,
                     qall_ref, mult_ref, sel_ref, m_ref, l_ref, acc_ref, *, seq):
    qt = pl.program_id(1)
    pos0 = qt * B_TQ
    n_sel = seq // SEL_BLOCK
    lane_g = _lane_group((B_TQ, B_SLAB))
    gmask = [lane_g == g for g in range(B_KV_HEADS)]
    rows_g = B_GROUP * B_TQ

    gates = _sigmoid(gate_ref[...]).astype(BF16)
    for c, gp_ref in enumerate((gp0_ref, gp1_ref, gp2_ref)):
        mult_ref[c] = _dot(gates, eg_ref[:, c * D_MODEL:(c + 1) * D_MODEL]) * _silu(gp_ref[...])

    for r in range(B_GROUP):
        slab = q_ref[:, r * B_SLAB:(r + 1) * B_SLAB] * SCALE
        for g in range(B_KV_HEADS):
            h = g * B_GROUP + r
            qall_ref[h * B_TQ:(h + 1) * B_TQ, :] = jnp.where(gmask[g], slab, 0.0).astype(BF16)

    o_ref[...] = jnp.zeros(o_ref.shape, F32)

    def emit(c, g, res):
        for r in range(B_GROUP):
            part = jnp.where(gmask[g], res[r * B_TQ:(r + 1) * B_TQ], 0.0)
            sl = slice(r * B_SLAB, (r + 1) * B_SLAB)
            o_ref[:, sl] += part * mult_ref[c, :, sl]

    qi = lax.broadcasted_iota(jnp.int32, (B_TQ, N_CMP_PAD), 0)
    ci = lax.broadcasted_iota(jnp.int32, (B_TQ, N_CMP_PAD), 1)

    kcmp = kcmp_ref[...].astype(BF16)
    vcmp = vcmp_ref[...].astype(BF16)
    dist_c = pos0 + qi - (ci * CMP_STRIDE + (CMP_LEN - 1))
    valid_c = dist_c >= 0
    dist_cf = dist_c.astype(F32)
    blk = lax.broadcasted_iota(jnp.int32, (n_sel, B_TQ), 0)
    posq_t = pos0 + lax.broadcasted_iota(jnp.int32, (n_sel, B_TQ), 1)
    cur = posq_t // SEL_BLOCK
    forced = (blk == 0) | (blk == cur) | (blk == cur - 1)
    reachable = blk * SEL_BLOCK <= posq_t
    for g in range(B_KV_HEADS):
        s_all = _dot_nt(qall_ref[g * rows_g:(g + 1) * rows_g, :], kcmp)
        probs = []
        for r in range(B_GROUP):
            s = s_all[r * B_TQ:(r + 1) * B_TQ] - slopes_ref[g, r] * dist_cf
            s = jnp.where(valid_c, s, NEG)
            m = jnp.max(s, axis=-1, keepdims=True)
            e = jnp.where(valid_c, jnp.exp(s - m), 0.0)
            den = jnp.maximum(jnp.sum(e, axis=-1, keepdims=True), 1e-30)
            probs.append(e / den)
        emit(0, g, _dot(jnp.concatenate(probs, axis=0).astype(BF16), vcmp))
        p_sum = probs[0] + probs[1] + probs[2] + probs[3]
        s_blk = _split_dot(p_sum, agg_ref[...])
        s_t = s_blk.T[0:n_sel, :]
        s_t = jnp.where(forced, s_t + FORCE_BONUS, s_t)
        s_t = jnp.where(reachable, s_t, NEG)
        sel_t = _rank_select(s_t, blk, n_sel)
        sel_t = jnp.concatenate([sel_t, jnp.zeros((N_SEL_PAD - n_sel, B_TQ), F32)], axis=0)
        sel_ref[g] = sel_t.T.astype(BF16)

    m_ref[...] = jnp.full(m_ref.shape, NEG, F32)
    l_ref[...] = jnp.zeros(l_ref.shape, F32)
    acc_ref[...] = jnp.zeros(acc_ref.shape, F32)
    qk = lax.broadcasted_iota(jnp.int32, (B_TQ, B_TK), 0)
    ck = lax.broadcasted_iota(jnp.int32, (B_TQ, B_TK), 1)
    eb = lax.broadcasted_iota(jnp.int32, (N_SEL_PAD, B_TK), 0)
    ek = lax.broadcasted_iota(jnp.int32, (N_SEL_PAD, B_TK), 1)

    def chunk(c, carry):
        k0 = pl.multiple_of(c * B_TK, B_TK)
        kc = kslc_ref[pl.ds(k0, B_TK), :].astype(BF16)
        vc = vslc_ref[pl.ds(k0, B_TK), :].astype(BF16)
        dist = pos0 + qk - (k0 + ck)
        causal = dist >= 0
        dist_f = dist.astype(F32)
        expand = jnp.where(eb == (k0 + ek) // SEL_BLOCK, 1.0, 0.0).astype(BF16)
        for g in range(B_KV_HEADS):
            valid = causal & (_dot(sel_ref[g], expand) > 0.5)
            s_all = _dot_nt(qall_ref[g * rows_g:(g + 1) * rows_g, :], kc)
            ps = []
            for r in range(B_GROUP):
                rows = pl.ds((g * B_GROUP + r) * B_TQ, B_TQ)
                s = s_all[r * B_TQ:(r + 1) * B_TQ] - slopes_ref[g, r] * dist_f
                s = jnp.where(valid, s, NEG)
                m_old = m_ref[rows, :]
                m_new = jnp.maximum(m_old, jnp.max(s, axis=-1, keepdims=True))
                alpha = jnp.exp(m_old - m_new)
                p = jnp.exp(s - jnp.concatenate([m_new, m_new], axis=1))
                l_ref[rows, :] = alpha * l_ref[rows, :] + jnp.sum(p, axis=-1, keepdims=True)
                m_ref[rows, :] = m_new
                acc_ref[rows, :] = acc_ref[rows, :] * jnp.concatenate([alpha, alpha], axis=1)
                ps.append(p.astype(BF16))
            rows4 = pl.ds(g * rows_g, rows_g)
            acc_ref[rows4, :] += _dot(jnp.concatenate(ps, axis=0), vc)
        return carry

    lax.fori_loop(0, (pos0 + B_TQ + B_TK - 1) // B_TK, chunk, 0)
    for g in range(B_KV_HEADS):
        rows4 = pl.ds(g * rows_g, rows_g)
        den = jnp.maximum(l_ref[rows4, :], 1e-30)
        emit(1, g, acc_ref[rows4, :] / jnp.concatenate([den, den], axis=1))

    w0 = pl.multiple_of(jnp.maximum(qt - WIN_B // B_TQ, 0) * B_TQ, B_TQ)
    qw = lax.broadcasted_iota(jnp.int32, (B_TQ, B_WIN_SPAN), 0)
    cw = lax.broadcasted_iota(jnp.int32, (B_TQ, B_WIN_SPAN), 1)
    dist_w = pos0 + qw - (w0 + cw)
    valid_w = (dist_w >= 0) & (dist_w <= WIN_B)
    dist_wm = jnp.where(valid_w, dist_w.astype(F32), MASKED_DIST)
    kw = kwin_ref[pl.ds(w0, B_WIN_SPAN), :].astype(BF16)
    vw = vwin_ref[pl.ds(w0, B_WIN_SPAN), :].astype(BF16)
    for g in range(B_KV_HEADS):
        s_all = _dot_nt(qall_ref[g * rows_g:(g + 1) * rows_g, :], kw)
        ps = []
        for r in range(B_GROUP):
            s = s_all[r * B_TQ:(r + 1) * B_TQ] - slopes_ref[g, r] * dist_wm
            m = jnp.max(s, axis=-1, keepdims=True)
            e = jnp.exp(s - m)
            den = jnp.maximum(jnp.sum(e, axis=-1, keepdims=True), 1e-30)
            ps.append((e / den).astype(BF16))
        emit(2, g, _dot(jnp.concatenate(ps, axis=0), vw))


def _b_prompt_attention(p3, cmp_kv, slopes):
    batch, seq, _ = p3.shape
    agg = _sel_aggregation((seq - CMP_LEN) // CMP_STRIDE + 1, seq // SEL_BLOCK)
    eg = _gate_expand_matrix()
    tile = lambda blk: pl.BlockSpec((None, B_TQ, D_MODEL), lambda b, t, blk=blk: (b, t, blk))
    kv = lambda kind: pl.BlockSpec((None, seq, B_SLAB),
                                   lambda b, t, blk=B_KV_COL // B_SLAB + kind: (b, 0, blk))
    fixed = lambda b, t: (0, 0)
    n_rows = N_HEADS * B_TQ
    out = pl.pallas_call(
        functools.partial(_b_prompt_kernel, seq=seq),
        grid=(batch, seq // B_TQ),
        in_specs=[pl.BlockSpec(memory_space=pltpu.SMEM),
                  tile(0), tile(1), tile(2), tile(3),
                  pl.BlockSpec((None, B_TQ, LANES), lambda b, t: (b, t, B_GATE_COL // LANES)),
                  kv(2), kv(3), kv(4), kv(5),
                  pl.BlockSpec((None, None, N_CMP_PAD, B_SLAB), lambda b, t: (0, b, 0, 0)),
                  pl.BlockSpec((None, None, N_CMP_PAD, B_SLAB), lambda b, t: (1, b, 0, 0)),
                  pl.BlockSpec((N_CMP_PAD, N_SEL_PAD), fixed),
                  pl.BlockSpec((LANES, 3 * D_MODEL), fixed)],
        out_specs=pl.BlockSpec((None, B_TQ, D_MODEL), lambda b, t: (b, t, 0)),
        scratch_shapes=[pltpu.VMEM((n_rows, B_SLAB), BF16),
                        pltpu.VMEM((3, B_TQ, D_MODEL), F32),
                        pltpu.VMEM((B_KV_HEADS, B_TQ, N_SEL_PAD), BF16),
                        pltpu.VMEM((n_rows, LANES), F32),
                        pltpu.VMEM((n_rows, LANES), F32),
                        pltpu.VMEM((n_rows, B_SLAB), F32)],
        out_shape=jax.ShapeDtypeStruct((batch, seq, D_MODEL), F32),
        compiler_params=_params(("parallel", "arbitrary")),
        name="b_prompt_attention",
    )(slopes, p3, p3, p3, p3, p3, p3, p3, p3, p3, cmp_kv, cmp_kv, agg, eg)
    return out.reshape(batch * seq, D_MODEL)


PAGE_ROWS = 128
PAGE_COLS = 4 * B_SLAB


def _page_copy(pages_ref, buf_ref, sem_ref, page, slot, p):
    return pltpu.make_async_copy(pages_ref.at[page],
                                 buf_ref.at[slot, :, pl.ds(p * PAGE_ROWS, PAGE_ROWS)],
                                 sem_ref.at[slot])


def _masked_softmax_rows(s, valid):
    s = jnp.where(valid, s, NEG)
    m = jnp.max(s, axis=-1, keepdims=True)
    e = jnp.where(valid, jnp.exp(s - m), 0.0)
    den = jnp.maximum(jnp.sum(e, axis=-1, keepdims=True), 1e-30)
    return e / den


def _b_sample_kernel(table_ref, pages_ref, q_ref, kvn_ref, gate_ref, gp_ref, win_ref,
                     slope_ref, w1t_ref, pos_ref, w1_ref, w2_ref, agg_ref, expand_ref, gb_ref,
                     o_ref, buf_ref, xs_ref, sem_ref, *, n_pages, page_base):
    i = pl.program_id(0)
    n = pl.num_programs(0)
    slot = i % 2
    n_past = n_pages * PAGE_ROWS
    n_sel = n_past // SEL_BLOCK + 1

    def start_fetch(seq_idx, to_slot):
        for p in range(n_pages):
            _page_copy(pages_ref, buf_ref, sem_ref, page_base + table_ref[seq_idx, p], to_slot, p).start()

    @pl.when(i == 0)
    def _():
        start_fetch(0, 0)

    @pl.when(i + 1 < n)
    def _():
        start_fetch(i + 1, 1 - slot)

    for p in range(n_pages):
        _page_copy(pages_ref, buf_ref, sem_ref, 0, slot, p).wait()

    for cb in range(2 * B_SLAB // LANES):
        xs_ref[cb] = buf_ref[slot, cb * LANES:(cb + 1) * LANES, :].T

    slope = slope_ref[:, 0:1]
    row_g = lax.broadcasted_iota(jnp.int32, (N_HEADS, B_SLAB), 0) // B_GROUP
    own = _lane_group((N_HEADS, B_SLAB)) == row_g

    hi = lax.broadcasted_iota(jnp.int32, (N_HEADS, B_GROUP), 0)
    ri = lax.broadcasted_iota(jnp.int32, (N_HEADS, B_GROUP), 1)
    spread = jnp.where(hi % B_GROUP == ri, 1.0, 0.0).astype(BF16)
    q_rows = _dot(spread, (q_ref[...] * SCALE).astype(BF16))
    q_all = jnp.where(own, q_rows, 0.0)
    q_bf = q_all.astype(BF16)

    def attend(keys_t, values_t, dist, k_new, v_new, valid):
        s = _dot(q_bf, keys_t.astype(BF16)) - slope * dist
        s_new = jnp.sum(q_all * k_new, axis=-1, keepdims=True)
        if valid is not None:
            s = jnp.where(valid, s, NEG)
        m = jnp.maximum(jnp.max(s, axis=-1, keepdims=True), s_new)
        e = jnp.exp(s - m)
        if valid is not None:
            e = jnp.where(valid, e, 0.0)
        e_new = jnp.exp(s_new - m)
        den = jnp.maximum(jnp.sum(e, axis=-1, keepdims=True) + e_new, 1e-30)
        return (_dot_nt(e.astype(BF16), values_t.astype(BF16)) + e_new * v_new) / den

    cmp_kv = []
    for kind in range(2):
        posterm = _dot(pos_ref[kind].astype(BF16), w1_ref[kind])
        load_rows = lambda l, half, kind=kind: xs_ref[2 * kind + half, pl.ds(l, N_CMP_PAD, stride=CMP_STRIDE), :]
        cmp_kv.append(_compress(load_rows, w1t_ref.at[kind], posterm, w2_ref[kind]))
    ci = lax.broadcasted_iota(jnp.int32, (N_HEADS, N_CMP_PAD), 1)
    dist_c = n_past - (ci * CMP_STRIDE + (CMP_LEN - 1))
    valid_c = dist_c >= 0
    s_c = _dot_nt(q_bf, cmp_kv[0].astype(BF16)) - slope * dist_c.astype(F32)
    p_c = _masked_softmax_rows(s_c, valid_c)
    o_cmp = _dot(p_c.astype(BF16), cmp_kv[1].astype(BF16))

    ha = lax.broadcasted_iota(jnp.int32, (N_HEADS, N_HEADS), 0) // B_GROUP
    hb = lax.broadcasted_iota(jnp.int32, (N_HEADS, N_HEADS), 1) // B_GROUP
    group_sum = jnp.where(ha == hb, 1.0, 0.0).astype(BF16)
    p_hi = p_c.astype(BF16)
    p_lo = (p_c - p_hi.astype(F32)).astype(BF16)
    p_grp = _dot(group_sum, p_hi) + _dot(group_sum, p_lo)
    s_blk = _split_dot(p_grp, agg_ref[...])
    blk = lax.broadcasted_iota(jnp.int32, (N_HEADS, N_SEL_PAD), 1)
    cur = n_past // SEL_BLOCK
    forced = (blk == 0) | (blk == cur) | (blk == cur - 1)
    s_blk = jnp.where(forced, s_blk + FORCE_BONUS, s_blk)
    s_blk = jnp.where(blk < n_sel, s_blk, NEG)
    rank = jnp.zeros(s_blk.shape, F32)
    for j in range(n_sel):
        col = s_blk[:, j:j + 1]
        ahead = (col > s_blk) | ((col == s_blk) & (blk > j))
        rank = rank + jnp.where(ahead, 1.0, 0.0)
    sel = jnp.where((rank < SEL_TOPK) & (blk < n_sel), 1.0, 0.0).astype(BF16)
    key_valid = _dot(sel, expand_ref[...]) > 0.5

    kslc_t = buf_ref[slot, 2 * B_SLAB:3 * B_SLAB, :]
    vslc_t = buf_ref[slot, 3 * B_SLAB:4 * B_SLAB, :]
    dist_s = (n_past - lax.broadcasted_iota(jnp.int32, (1, n_past), 1)).astype(F32)
    o_sel = attend(kslc_t, vslc_t, dist_s, kvn_ref[2:3, :], kvn_ref[3:4, :], key_valid)

    kwin_t = win_ref[0]
    vwin_t = win_ref[1]
    n_buf = kwin_t.shape[1]
    dist_w = (n_buf - lax.broadcasted_iota(jnp.int32, (1, n_buf), 1)).astype(F32)
    o_win = attend(kwin_t, vwin_t, dist_w, kvn_ref[4:5, :], kvn_ref[5:6, :], None)

    crow = lax.broadcasted_iota(jnp.int32, (3 * B_GROUP, LANES), 0)
    ccol = lax.broadcasted_iota(jnp.int32, (3 * B_GROUP, LANES), 1)
    mine = (ccol // N_HEADS == crow // B_GROUP) & (ccol % B_GROUP == crow % B_GROUP) & (ccol < 3 * N_HEADS)
    gate_rows = jnp.where(mine, _sigmoid(gate_ref[...]), 0.0).astype(BF16)
    mult = _dot(gate_rows, gb_ref[...]) * _silu(gp_ref[...])
    gather = jnp.where(lax.broadcasted_iota(jnp.int32, (B_GROUP, N_HEADS), 1) % B_GROUP
                       == lax.broadcasted_iota(jnp.int32, (B_GROUP, N_HEADS), 0), 1.0, 0.0).astype(BF16)
    out = jnp.zeros((B_GROUP, B_SLAB), F32)
    for c, res in enumerate((o_cmp, o_sel, o_win)):
        res = jnp.where(own, res, 0.0)
        r_hi = res.astype(BF16)
        r_lo = (res - r_hi.astype(F32)).astype(BF16)
        out = out + (_dot(gather, r_hi) + _dot(gather, r_lo)) * mult[c * B_GROUP:(c + 1) * B_GROUP]
    o_ref[...] = out


def _b_sample_attention(proj_s, pages, page_table, win_t, layer, slopes_rep, cw):
    n = proj_s.shape[0]
    n_pages = page_table.shape[1]
    n_past = n_pages * PAGE_ROWS
    pool = pages.shape[1]
    n_buf = win_t.shape[3]
    w1t, pos, w1, w2wide = cw
    agg = _sel_aggregation((n_past + 1 - CMP_LEN) // CMP_STRIDE + 1, n_past // SEL_BLOCK + 1)
    expand = (np.arange(N_SEL_PAD)[:, None] == (np.arange(n_past) // SEL_BLOCK)[None, :]).astype(np.float32)
    group_bcast = (np.arange(LANES)[:, None] % N_HEADS // B_GROUP
                   == (np.arange(B_SLAB) // HEAD_DIM)[None, :]).astype(np.float32)
    pages_t = jnp.transpose(pages, (0, 1, 3, 4, 5, 2)).reshape(pages.shape[0] * pool, PAGE_COLS, PAGE_ROWS)
    q3 = proj_s[:, B_Q_COL:B_Q_COL + D_MODEL].reshape(n, B_GROUP, B_SLAB)
    gp3 = proj_s[:, B_GPATH_COL:B_GPATH_COL + 3 * D_MODEL].reshape(n, 3 * B_GROUP, B_SLAB)
    kvn = proj_s[:, B_KV_COL:B_KV_COL + 6 * B_SLAB].reshape(n, 6, B_SLAB)
    gate3 = proj_s[:, B_GATE_COL:B_GATE_COL + LANES].reshape(n, 1, LANES)
    per_seq = lambda i, t: (i, 0, 0)
    fixed2 = lambda i, t: (0, 0)
    fixed3 = lambda i, t: (0, 0, 0)
    win_spec = pl.BlockSpec((None, 2, B_SLAB, n_buf), lambda i, t, base=layer * n: (base + i, 0, 0, 0))
    in_specs = [pl.BlockSpec(memory_space=pl.ANY),
                pl.BlockSpec((None, B_GROUP, B_SLAB), per_seq),
                pl.BlockSpec((None, 6, B_SLAB), per_seq),
                pl.BlockSpec((None, 1, LANES), per_seq),
                pl.BlockSpec((None, 3 * B_GROUP, B_SLAB), per_seq),
                win_spec,
                pl.BlockSpec((N_HEADS, LANES), fixed2),
                pl.BlockSpec(w1t.shape, lambda i, t: (0, 0, 0, 0)),
                pl.BlockSpec(pos.shape, fixed3),
                pl.BlockSpec(w1.shape, fixed3),
                pl.BlockSpec(w2wide.shape, fixed3),
                pl.BlockSpec((N_CMP_PAD, N_SEL_PAD), fixed2),
                pl.BlockSpec((N_SEL_PAD, n_past), fixed2),
                pl.BlockSpec((LANES, B_SLAB), fixed2)]
    args = [page_table, pages_t, q3, kvn, gate3, gp3, win_t, slopes_rep, w1t, pos, w1, w2wide, agg,
            jnp.asarray(expand, BF16), jnp.asarray(group_bcast, BF16)]
    grid_spec = pltpu.PrefetchScalarGridSpec(
        num_scalar_prefetch=1,
        grid=(n,),
        in_specs=in_specs,
        out_specs=pl.BlockSpec((None, B_GROUP, B_SLAB), per_seq),
        scratch_shapes=[pltpu.VMEM((2, PAGE_COLS, n_past), F32),
                        pltpu.VMEM((2 * B_SLAB // LANES, n_past, LANES), F32),
                        pltpu.SemaphoreType.DMA((2,))],
    )
    out = pl.pallas_call(
        functools.partial(_b_sample_kernel, n_pages=n_pages, page_base=layer * pool),
        grid_spec=grid_spec,
        out_shape=jax.ShapeDtypeStruct((n, B_GROUP, B_SLAB), F32),
        compiler_params=_params(("arbitrary",)),
        name="b_sample_attention",
    )(*args)
    return out.reshape(n, D_MODEL)


WIN_ADV_BLOCK = 4


def _advance_window_kernel(win_ref, newt_ref, o_ref):
    i = pl.program_id(1)
    for s in range(WIN_ADV_BLOCK):
        for kv in range(2):
            o_ref[s, kv] = _advance_rows(win_ref[s, kv], _lane_column(newt_ref[kv], i * WIN_ADV_BLOCK + s))


def _advance_window(win_t, new_t):
    layers, _, feats, n = new_t.shape
    n_buf = win_t.shape[3]
    steps = n // WIN_ADV_BLOCK
    win_spec = pl.BlockSpec((WIN_ADV_BLOCK, 2, feats, n_buf), lambda l, i: (l * steps + i, 0, 0, 0))
    return pl.pallas_call(
        _advance_window_kernel,
        grid=(layers, steps),
        in_specs=[win_spec, pl.BlockSpec((None, 2, feats, n), lambda l, i: (l, 0, 0, 0))],
        out_specs=win_spec,
        out_shape=jax.ShapeDtypeStruct(win_t.shape, F32),
        compiler_params=_params(("parallel", "arbitrary")),
        name="b_advance_window",
    )(win_t, new_t)


PROJ_TM = 1024
LN_TM = 512


STATE_ROWS = 1024
STATE_FEATS = 512


def _state_kernel(*refs):
    x_ref, o_ref = refs[0], refs[-1]

    @pl.when(pl.program_id(0) == 0)
    def _():
        o_ref[...] = x_ref[...].T

    @pl.when(pl.program_id(0) > 0)
    def _():
        o_ref[...] = jnp.zeros(o_ref.shape, F32)


def _fill_phases(prev, layers):
    return 1 if prev is not None else layers


def _emit_state(proj3, col0, n_feat, n_rows, prev, layer, layers):
    batch, seq, _ = proj3.shape
    tr = min(STATE_ROWS, n_rows)
    r0 = (seq - n_rows) // tr
    c0 = col0 // STATE_FEATS
    n_f, n_t = n_feat // STATE_FEATS, n_rows // tr

    def in_map(p, b, f, t):
        live = p == 0
        return (jnp.where(live, b, batch - 1), r0 + jnp.where(live, t, n_t - 1), c0 + jnp.where(live, f, n_f - 1))

    in_specs = [pl.BlockSpec((None, tr, STATE_FEATS), in_map)]
    args = [proj3]
    aliases = {}
    if prev is not None:
        in_specs.append(pl.BlockSpec(memory_space=pl.ANY))
        args.append(prev)
        aliases = {1: 0}
    return pl.pallas_call(
        _state_kernel,
        grid=(_fill_phases(prev, layers), batch, n_f, n_t),
        in_specs=in_specs,
        out_specs=pl.BlockSpec((None, STATE_FEATS, tr),
                               lambda p, b, f, t: (((layer + p) % layers) * batch + b, f, t)),
        out_shape=jax.ShapeDtypeStruct((layers * batch, n_feat, n_rows), F32),
        input_output_aliases=aliases,
        compiler_params=_params(("arbitrary", "arbitrary", "arbitrary", "arbitrary")),
        name="emit_state",
    )(*args)


def kernel(x_prompt, x_sample, cache_a_w128, cache_a_w512, cache_a_w2048, cache_b_pages, cache_b_win,
           page_table, ln_g, ln_b, a_w_in, a_w_out, b_w_in, b_w_out, b_phi_pos, b_phi_w1, b_phi_w2):
    batch, seq, _ = x_prompt.shape
    n = x_sample.shape[0]
    assert x_sample.shape[1] == 1 and seq % (A_TILE * A_GROUPS[-1][1]) == 0
    a_caches = (cache_a_w128, cache_a_w512, cache_a_w2048)
    for (win, dil), c in zip(A_GROUPS, a_caches):
        assert c.shape[2] == win and win == A_CLASS_WINDOW * dil
    assert cache_b_win.shape[2] == WIN_B and seq >= WIN_B

    a_slopes = jnp.asarray(_alibi_slopes(3 * N_HEADS).reshape(N_HEADS, 3).T, F32)
    b_slopes = jnp.asarray(_alibi_slopes(N_HEADS).reshape(B_KV_HEADS, B_GROUP), F32)
    b_slopes_rep = jnp.asarray(np.repeat(_alibi_slopes(N_HEADS)[:, None], LANES, axis=1), F32)
    b_order, head_perm = _b_column_order()

    yp = x_prompt.reshape(batch * seq, D_MODEL)
    ys = x_sample.reshape(n, D_MODEL)
    n_a = len(a_caches[0])
    n_b = len(cache_b_win)
    a_caches_t = [_feature_major(c) for c in a_caches]
    b_win_t = _feature_major(cache_b_win)
    a_adv_t = None
    a_p_t = [None for _ in A_GROUPS]
    rows_p_t = None
    win_p_t = None
    rows_s, win_new_t = [], []
    for i in range(DEPTH):
        j = i // 2
        if i % 2 == 0:
            w_in = a_w_in[j].astype(BF16)
            w_out = a_w_out[j].astype(BF16)
            proj_p = _matmul(yp, w_in, tm=PROJ_TM, tn=D_MODEL, name=f"a_proj_prompt_{j}")
            proj_s = _matmul(ys, w_in, tm=n, tn=D_MODEL, name=f"a_proj_sample_{j}")
            o_p = _a_prompt_attention(proj_p, a_slopes, batch, seq)
            o_s, a_adv_t = _a_sample_attention(proj_s, a_caches_t, a_adv_t, j, a_slopes)
            pp = proj_p.reshape(batch, seq, -1)
            for g, (win, dil) in enumerate(A_GROUPS):
                a_p_t[g] = _emit_state(pp, g * 3 * D_MODEL + D_MODEL, 2 * D_MODEL, min(win, seq),
                                       a_p_t[g], j, n_a)
        else:
            w_in_f = b_w_in[j]
            w_in = jnp.where(jnp.asarray(b_order >= 0)[None, :], w_in_f[:, np.maximum(b_order, 0)], 0.0).astype(BF16)
            w_out = b_w_out[j][head_perm, :].astype(BF16)
            proj_p = _matmul(yp, w_in, tm=PROJ_TM, tn=B_COLS // 5, name=f"b_proj_prompt_{j}")
            proj_s = _matmul(ys, w_in, tm=n, tn=B_COLS // 5, name=f"b_proj_sample_{j}")
            p3 = proj_p.reshape(batch, seq, B_COLS)
            cw = _compress_weights(b_phi_pos[j], b_phi_w1[j], b_phi_w2[j])
            cmp_kv = _b_prompt_compress(p3, *cw)
            o_p = _b_prompt_attention(p3, cmp_kv, b_slopes)
            o_s = _b_sample_attention(proj_s, cache_b_pages, page_table, b_win_t, j, b_slopes_rep, cw)
            rows_p_t = _emit_state(p3, B_KV_COL, 4 * B_SLAB, seq, rows_p_t, j, n_b)
            win_p_t = _emit_state(p3, B_KV_COL + 4 * B_SLAB, 2 * B_SLAB, WIN_B, win_p_t, j, n_b)
            rows_s.append(proj_s[:, B_KV_COL:B_KV_COL + 4 * B_SLAB].reshape(n, 1, 4, B_KV_HEADS, HEAD_DIM))
            win_new_t.append(proj_s[:, B_KV_COL + 4 * B_SLAB:B_KV_COL + 6 * B_SLAB].T.reshape(2, B_SLAB, n))
        yp = _outproj_ln(o_p, yp, w_out, ln_g[i], ln_b[i], tm=LN_TM, name=f"outproj_ln_prompt_{i}")
        ys = _outproj_ln(o_s, ys, w_out, ln_g[i], ln_b[i], tm=n, name=f"outproj_ln_sample_{i}")

    def states(ct, kinds, layers, heads):
        ln, feats, rows = ct.shape
        return _row_major(ct.reshape(ln, kinds, feats // kinds, rows), layers, heads)

    a_s = [_row_major(ct, n_a, N_HEADS) for ct in a_adv_t]
    a_p = [states(ct, 2, n_a, N_HEADS) for ct in a_p_t]
    return (yp.reshape(batch, seq, D_MODEL), ys.reshape(n, 1, D_MODEL),
            a_p[0], a_s[0], a_p[1], a_s[1], a_p[2], a_s[2],
            states(rows_p_t, 4, n_b, B_KV_HEADS), jnp.stack(rows_s),
            states(win_p_t, 2, n_b, B_KV_HEADS),
            _row_major(_advance_window(b_win_t, jnp.stack(win_new_t)), n_b, B_KV_HEADS))
```
